```python
import math
import jax
import jax.numpy as jnp
from jax import lax
import numpy as np

D_MODEL = 1024
BATCH = 8
SEQ = 2048
DEPTH = 2
DEC_BATCH = 16
DEC_SEQ = 16
PAST_LEN = 1024

CHUNK = 64
EPS = 1e-6
D_FF = 2816
RET_HEADS = 8
RET_DK = 64
RET_DV = 64
MLA_HEADS = 8
MLA_Q_LORA = 384
MLA_KV_LORA = 256
MLA_NOPE = 64
MLA_ROPE = 32
MLA_DQK = MLA_NOPE + MLA_ROPE
MLA_DV = 64
ROPE_BASE = 10000.0
Q_BLOCK = 128
SSM_HEADS = 16
SSM_HEADDIM = 64
SSM_D_INNER = SSM_HEADS * SSM_HEADDIM
SSM_GROUPS = 2
SSM_STATE = 128
CONV_W = 4
CONV_DIM = SSM_D_INNER + 2 * SSM_GROUPS * SSM_STATE
D_MIX = RET_HEADS * RET_DV + MLA_HEADS * MLA_DV + SSM_D_INNER
RET_COLS = 2 * RET_HEADS * RET_DK + 2 * RET_HEADS * RET_DV
MLA_COLS = MLA_Q_LORA + MLA_KV_LORA + MLA_ROPE
SSM_COLS = SSM_D_INNER + CONV_DIM + SSM_HEADS
IN_COLS = RET_COLS + MLA_COLS + SSM_COLS

kernel_name = 'hymba_style_retention_mla_ssd_streaming_step'


def rmsnorm(x, g):
    xf = x.astype(jnp.float32)
    y = xf * lax.rsqrt(jnp.mean(xf * xf, axis=-1, keepdims=True) + EPS)
    return (y * g.astype(jnp.float32)).astype(x.dtype)


def swiglu(h, wgu, wd):
    gate, up = jnp.split(h @ wgu, 2, axis=-1)
    return (jax.nn.silu(gate) * up) @ wd


def rope(x, pos, theta):
    ang = pos.astype(jnp.float32)[:, None] * theta[None, :]
    cos = jnp.cos(ang)[:, None, :]
    sin = jnp.sin(ang)[:, None, :]
    x1, x2 = jnp.split(x.astype(jnp.float32), 2, axis=-1)
    return jnp.concatenate([x1 * cos - x2 * sin, x1 * sin + x2 * cos], axis=-1).astype(x.dtype)


def ret_theta():
    return 1.0 / (10000.0 ** jnp.linspace(0.0, 1.0, RET_DK // 2, dtype=jnp.float32))


def mla_theta():
    return 1.0 / (ROPE_BASE ** (jnp.arange(0, MLA_ROPE, 2, dtype=jnp.float32) / MLA_ROPE))


def retention_scan(q, k, v, s0):
    bsz, t, h, _ = q.shape
    c = min(CHUNK, t)
    n = t // c
    log_g = jnp.log(1.0 - 2.0 ** (-5.0 - jnp.arange(h, dtype=jnp.float32)))
    idx = jnp.arange(c, dtype=jnp.float32)
    diff = idx[:, None] - idx[None, :]
    dmask = jnp.where(diff >= 0, jnp.exp(log_g[:, None, None] * jnp.maximum(diff, 0.0)), 0.0)
    q_dec = jnp.exp(log_g[:, None] * (idx[None, :] + 1.0))
    k_dec = jnp.exp(log_g[:, None] * (c - 1.0 - idx[None, :]))
    c_dec = jnp.exp(log_g * c)

    def to_chunks(a):
        return a.astype(jnp.float32).reshape(bsz, n, c, h, a.shape[-1]).transpose(1, 0, 3, 2, 4)

    qc, kc, vc = to_chunks(q), to_chunks(k) * RET_DK ** -0.5, to_chunks(v)

    def step(s, inp):
        qi, ki, vi = inp
        att = jnp.einsum('bhqd,bhkd->bhqk', qi, ki) * dmask
        o = (jnp.einsum('bhqk,bhkv->bhqv', att, vi)
             + jnp.einsum('bhqd,bhdv->bhqv', qi, s) * q_dec[None, :, :, None])
        s = c_dec[None, :, None, None] * s + jnp.einsum('bhkd,bhkv->bhdv', ki * k_dec[None, :, :, None], vi)
        return s, o

    s_new, o = lax.scan(step, s0.astype(jnp.float32), (qc, kc, vc))
    return o.transpose(1, 0, 3, 2, 4).reshape(bsz, t, h, -1), s_new


def retention_group(cols, pos, s0, g_norm):
    bsz, t, _ = cols.shape
    hk = RET_HEADS * RET_DK
    hv = RET_HEADS * RET_DV
    q, k, v, gate = jnp.split(cols, [hk, 2 * hk, 2 * hk + hv], axis=-1)
    theta = ret_theta()
    q = rope(q.reshape(bsz, t, RET_HEADS, RET_DK), pos, theta)
    k = rope(k.reshape(bsz, t, RET_HEADS, RET_DK), pos, theta)
    v = v.reshape(bsz, t, RET_HEADS, RET_DV)
    o, s_new = retention_scan(q, k, v, s0)
    o = rmsnorm(o.astype(cols.dtype), g_norm).reshape(bsz, t, hv)
    return o * jax.nn.silu(gate), s_new


def rope_tail(x, pos, theta):
    return jnp.concatenate([x[..., :MLA_NOPE], rope(x[..., MLA_NOPE:], pos, theta)], axis=-1)


def block_attend(q, k, v, q_pos, k_pos):
    s = jnp.einsum('bqhd,bkhd->bhqk', q, k).astype(jnp.float32) * MLA_DQK ** -0.5
    visible = (k_pos[None, :] // CHUNK) <= (q_pos[:, None] // CHUNK)
    s = jnp.where(visible[None, None], s, -jnp.inf)
    p = jax.nn.softmax(s, axis=-1).astype(v.dtype)
    return jnp.einsum('bhqk,bkhv->bqhv', p, v)


def mla_attend(q, k, v, q_pos, k_pos):
    bsz, t, h, d = q.shape
    if t > Q_BLOCK and t % Q_BLOCK == 0:
        nb = t // Q_BLOCK
        qb = q.reshape(bsz, nb, Q_BLOCK, h, d).transpose(1, 0, 2, 3, 4)
        pb = q_pos.reshape(nb, Q_BLOCK)
        o = lax.map(lambda a: block_attend(a[0], k, v, a[1], k_pos), (qb, pb))
        return o.transpose(1, 0, 2, 3, 4).reshape(bsz, t, h, -1)
    return block_attend(q, k, v, q_pos, k_pos)


def mla_group(cols, pos, ckv_past, krope_past, past_pos, q_norm, kv_norm, w_uq, w_ukv, q_gain, k_gain):
    bsz, t, _ = cols.shape
    c_q, c_kv, k_rope = jnp.split(cols, [MLA_Q_LORA, MLA_Q_LORA + MLA_KV_LORA], axis=-1)
    c_q = rmsnorm(c_q, q_norm)
    c_kv = rmsnorm(c_kv, kv_norm)
    theta = mla_theta()
    q = (c_q @ w_uq).reshape(bsz, t, MLA_HEADS, MLA_DQK)
    q = rope_tail(rmsnorm(q, q_gain), pos, theta)
    if ckv_past is None:
        ckv_all, krope_all, k_pos = c_kv, k_rope, pos
    else:
        ckv_all = jnp.concatenate([ckv_past.astype(c_kv.dtype), c_kv], axis=1)
        krope_all = jnp.concatenate([krope_past.astype(k_rope.dtype), k_rope], axis=1)
        k_pos = jnp.concatenate([past_pos, pos])
    s = ckv_all.shape[1]
    kv = (ckv_all @ w_ukv).reshape(bsz, s, MLA_HEADS, MLA_NOPE + MLA_DV)
    k_nope, v = jnp.split(kv, [MLA_NOPE], axis=-1)
    k = jnp.concatenate([k_nope, jnp.broadcast_to(krope_all[:, :, None, :], (bsz, s, MLA_HEADS, MLA_ROPE))], axis=-1)
    k = rope_tail(rmsnorm(k, k_gain), k_pos, theta)
    o = mla_attend(q, k, v, pos, k_pos)
    return o.reshape(bsz, t, MLA_HEADS * MLA_DV), c_kv, k_rope


def ssd_scan(x, dt, a, bm, cm, h0):
    bsz, t, h, p = x.shape
    c = min(CHUNK, t)
    n = t // c
    rep = h // SSM_GROUPS
    bh = jnp.repeat(bm, rep, axis=2)
    ch = jnp.repeat(cm, rep, axis=2)

    def to_chunks(u):
        return jnp.swapaxes(u.astype(jnp.float32).reshape((bsz, n, c) + u.shape[2:]), 0, 1)

    causal = jnp.arange(c)[:, None] >= jnp.arange(c)[None, :]

    def step(hs, inp):
        xi, dti, bi, ci = inp
        acum = jnp.cumsum(dti * a, axis=1)
        seg = acum[:, :, None, :] - acum[:, None, :, :]
        decay = jnp.exp(jnp.where(causal[None, :, :, None], seg, -jnp.inf))
        xdt = xi * dti[..., None]
        scores = jnp.einsum('bthn,bshn->btsh', ci, bi) * decay
        y = (jnp.einsum('btsh,bshp->bthp', scores, xdt)
             + jnp.einsum('bthn,bhpn->bthp', ci, hs) * jnp.exp(acum)[..., None])
        alast = acum[:, -1]
        w = jnp.exp(alast[:, None, :] - acum)
        hs = jnp.exp(alast)[:, :, None, None] * hs + jnp.einsum('bshp,bshn->bhpn', xdt * w[..., None], bi)
        return hs, y

    h_new, y = lax.scan(step, h0.astype(jnp.float32), (to_chunks(x), to_chunks(dt), to_chunks(bh), to_chunks(ch)))
    return jnp.swapaxes(y, 0, 1).reshape(bsz, t, h, p), h_new


def ssm_group(cols, conv_buf, h0, conv_w, conv_b, dt_bias, a_log, d_skip, norm_g):
    bsz, t, _ = cols.shape
    z, xbc, dt = jnp.split(cols, [SSM_D_INNER, SSM_D_INNER + CONV_DIM], axis=-1)
    xpad = jnp.concatenate([conv_buf.astype(xbc.dtype), xbc], axis=1)
    new_buf = xpad[:, xpad.shape[1] - (CONV_W - 1):]
    conv = lax.conv_general_dilated(xpad, conv_w[:, None, :], window_strides=(1,), padding='VALID',
                                    dimension_numbers=('NWC', 'WIO', 'NWC'), feature_group_count=CONV_DIM)
    xbc = jax.nn.silu(conv + conv_b)
    xs, bm, cm = jnp.split(xbc, [SSM_D_INNER, SSM_D_INNER + SSM_GROUPS * SSM_STATE], axis=-1)
    dt = jax.nn.softplus(dt.astype(jnp.float32) + dt_bias.astype(jnp.float32))
    a = -jnp.exp(a_log.astype(jnp.float32))
    xs = xs.reshape(bsz, t, SSM_HEADS, SSM_HEADDIM)
    y, h_new = ssd_scan(xs, dt, a, bm.reshape(bsz, t, SSM_GROUPS, SSM_STATE),
                        cm.reshape(bsz, t, SSM_GROUPS, SSM_STATE), h0)
    y = (y + d_skip.astype(jnp.float32)[:, None] * xs.astype(jnp.float32)).astype(cols.dtype)
    gated = (y.reshape(bsz, t, SSM_D_INNER) * jax.nn.silu(z)).reshape(bsz, t, SSM_GROUPS, SSM_D_INNER // SSM_GROUPS)
    out = rmsnorm(gated, norm_g).reshape(bsz, t, SSM_D_INNER)
    return out, h_new, new_buf


def layer(x, pos, ckv_past, krope_past, past_pos, ret_s0, ssm_h0, conv_buf, p):
    x = x + 0.5 * swiglu(rmsnorm(x, p['ffn1_norm']), p['ffn1_wgu'], p['ffn1_wd'])
    cols = rmsnorm(x, p['mix_norm']) @ p['w_in']
    ret_cols, mla_cols, ssm_cols = jnp.split(cols, [RET_COLS, RET_COLS + MLA_COLS], axis=-1)
    o_ret, ret_s = retention_group(ret_cols, pos, ret_s0, p['ret_norm'])
    o_mla, c_kv, k_rope = mla_group(mla_cols, pos, ckv_past, krope_past, past_pos, p['mla_q_norm'],
                                    p['mla_kv_norm'], p['mla_w_uq'], p['mla_w_ukv'], p['mla_q_gain'], p['mla_k_gain'])
    o_ssm, ssm_h, conv_new = ssm_group(ssm_cols, conv_buf, ssm_h0, p['ssm_conv_w'], p['ssm_conv_b'],
                                       p['ssm_dt_bias'], p['ssm_a_log'], p['ssm_d'], p['ssm_norm'])
    x = x + jnp.concatenate([o_ret, o_mla, o_ssm], axis=-1) @ p['w_out']
    x = x + 0.5 * swiglu(rmsnorm(x, p['ffn2_norm']), p['ffn2_wgu'], p['ffn2_wd'])
    return x, (c_kv, k_rope, ret_s.astype(x.dtype), ssm_h.astype(x.dtype), conv_new)


def setup_inputs(seed: int = 0) -> dict:
    key = jax.random.key(seed)
    ks = iter(jax.random.split(key, 48))

    def nrm(shape, scale):
        return scale * jax.random.normal(next(ks), shape, jnp.float32)

    def gain(shape):
        return 1.0 + nrm(shape, 0.02)

    dt0 = jnp.exp(jax.random.uniform(next(ks), (DEPTH, SSM_HEADS), jnp.float32, math.log(1e-3), math.log(1e-1)))
    dt_bias = dt0 + jnp.log(-jnp.expm1(-dt0))
    a_log = jnp.log(jax.random.uniform(next(ks), (DEPTH, SSM_HEADS), jnp.float32, 1.0, 16.0))
    return {
        'x_prompt': nrm((BATCH, SEQ, D_MODEL), 1.0),
        'x_sample': nrm((DEC_BATCH, DEC_SEQ, D_MODEL), 1.0),
        'cache_mla_ckv': nrm((DEPTH, DEC_BATCH, PAST_LEN, MLA_KV_LORA), 1.0),
        'cache_mla_krope': nrm((DEPTH, DEC_BATCH, PAST_LEN, MLA_ROPE), 1.0),
        'state_ret': nrm((DEPTH, DEC_BATCH, RET_HEADS, RET_DK, RET_DV), 0.3),
        'state_ssm': nrm((DEPTH, DEC_BATCH, SSM_HEADS, SSM_HEADDIM, SSM_STATE), 0.3),
        'state_conv': nrm((DEPTH, DEC_BATCH, CONV_W - 1, CONV_DIM), 1.0),
        'ffn1_norm': gain((DEPTH, D_MODEL)),
        'ffn1_wgu': nrm((DEPTH, D_MODEL, 2 * D_FF), D_MODEL ** -0.5),
        'ffn1_wd': nrm((DEPTH, D_FF, D_MODEL), D_FF ** -0.5),
        'mix_norm': gain((DEPTH, D_MODEL)),
        'w_in': nrm((DEPTH, D_MODEL, IN_COLS), D_MODEL ** -0.5),
        'ret_norm': gain((DEPTH, RET_HEADS, RET_DV)),
        'mla_q_norm': gain((DEPTH, MLA_Q_LORA)),
        'mla_kv_norm': gain((DEPTH, MLA_KV_LORA)),
        'mla_w_uq': nrm((DEPTH, MLA_Q_LORA, MLA_HEADS * MLA_DQK), MLA_Q_LORA ** -0.5),
        'mla_w_ukv': nrm((DEPTH, MLA_KV_LORA, MLA_HEADS * (MLA_NOPE + MLA_DV)), MLA_KV_LORA ** -0.5),
        'mla_q_gain': gain((DEPTH, MLA_DQK)),
        'mla_k_gain': gain((DEPTH, MLA_DQK)),
        'ssm_conv_w': nrm((DEPTH, CONV_W, CONV_DIM), CONV_W ** -0.5),
        'ssm_conv_b': nrm((DEPTH, CONV_DIM), 0.02),
        'ssm_dt_bias': dt_bias,
        'ssm_a_log': a_log,
        'ssm_d': gain((DEPTH, SSM_HEADS)),
        'ssm_norm': gain((DEPTH, SSM_GROUPS, SSM_D_INNER // SSM_GROUPS)),
        'w_out': nrm((DEPTH, D_MIX, D_MODEL), D_MIX ** -0.5),
        'ffn2_norm': gain((DEPTH, D_MODEL)),
        'ffn2_wgu': nrm((DEPTH, D_MODEL, 2 * D_FF), D_MODEL ** -0.5),
        'ffn2_wd': nrm((DEPTH, D_FF, D_MODEL), D_FF ** -0.5),
    }


def reference(x_prompt, x_sample, cache_mla_ckv, cache_mla_krope, state_ret, state_ssm, state_conv,
              ffn1_norm, ffn1_wgu, ffn1_wd, mix_norm, w_in, ret_norm, mla_q_norm, mla_kv_norm,
              mla_w_uq, mla_w_ukv, mla_q_gain, mla_k_gain, ssm_conv_w, ssm_conv_b, ssm_dt_bias,
              ssm_a_log, ssm_d, ssm_norm, w_out, ffn2_norm, ffn2_wgu, ffn2_wd):
    b_p, t_p, _ = x_prompt.shape
    t_s = x_sample.shape[1]
    past = cache_mla_ckv.shape[2]
    pos_p = jnp.arange(t_p, dtype=jnp.int32)
    past_pos = jnp.arange(past, dtype=jnp.int32)
    pos_s = past + jnp.arange(t_s, dtype=jnp.int32)
    yp, ys = x_prompt, x_sample
    new_p = [[], [], [], [], []]
    new_s = [[], [], [], [], []]
    for l in range(DEPTH):
        p = {
            'ffn1_norm': ffn1_norm[l], 'ffn1_wgu': ffn1_wgu[l], 'ffn1_wd': ffn1_wd[l],
            'mix_norm': mix_norm[l], 'w_in': w_in[l], 'ret_norm': ret_norm[l],
            'mla_q_norm': mla_q_norm[l], 'mla_kv_norm': mla_kv_norm[l], 'mla_w_uq': mla_w_uq[l],
            'mla_w_ukv': mla_w_ukv[l], 'mla_q_gain': mla_q_gain[l], 'mla_k_gain': mla_k_gain[l],
            'ssm_conv_w': ssm_conv_w[l], 'ssm_conv_b': ssm_conv_b[l], 'ssm_dt_bias': ssm_dt_bias[l],
            'ssm_a_log': ssm_a_log[l], 'ssm_d': ssm_d[l], 'ssm_norm': ssm_norm[l], 'w_out': w_out[l],
            'ffn2_norm': ffn2_norm[l], 'ffn2_wgu': ffn2_wgu[l], 'ffn2_wd': ffn2_wd[l],
        }
        yp, st_p = layer(yp, pos_p, None, None, None,
                         jnp.zeros((b_p, RET_HEADS, RET_DK, RET_DV), jnp.float32),
                         jnp.zeros((b_p, SSM_HEADS, SSM_HEADDIM, SSM_STATE), jnp.float32),
                         jnp.zeros((b_p, CONV_W - 1, CONV_DIM), x_prompt.dtype), p)
        ys, st_s = layer(ys, pos_s, cache_mla_ckv[l], cache_mla_krope[l], past_pos,
                         state_ret[l], state_ssm[l], state_conv[l], p)
        for i in range(5):
            new_p[i].append(st_p[i])
            new_s[i].append(st_s[i])
    ckv_p = jnp.stack(new_p[0])
    krope_p = jnp.stack(new_p[1])
    ret_p = jnp.stack(new_p[2])
    ssm_p = jnp.stack(new_p[3])
    conv_p = jnp.stack(new_p[4])
    ckv_s = jnp.stack(new_s[0])
    krope_s = jnp.stack(new_s[1])
    ret_s = jnp.stack(new_s[2])
    ssm_s = jnp.stack(new_s[3])
    conv_s = jnp.stack(new_s[4])
    return (yp, ys, ckv_p, krope_p, ret_p, ssm_p, conv_p, ckv_s, krope_s, ret_s, ssm_s, conv_s)
```

```python
import functools

import numpy as np
import jax
import jax.numpy as jnp
from jax import lax
from jax.experimental import pallas as pl
from jax.experimental.pallas import tpu as pltpu

D_MODEL = 1024
D_FF = 2816
CHUNK = 64
EPS = 1e-6
RET_HEADS = 8
RET_DK = 64
RET_DV = 64
MLA_HEADS = 8
MLA_Q_LORA = 384
MLA_KV_LORA = 256
MLA_NOPE = 64
MLA_ROPE = 32
MLA_DQK = MLA_NOPE + MLA_ROPE
MLA_DV = 64
ROPE_BASE = 10000.0
SSM_HEADS = 16
SSM_HEADDIM = 64
SSM_D_INNER = SSM_HEADS * SSM_HEADDIM
SSM_GROUPS = 2
SSM_STATE = 128
CONV_W = 4
CONV_DIM = SSM_D_INNER + 2 * SSM_GROUPS * SSM_STATE
RET_COLS = 2 * RET_HEADS * RET_DK + 2 * RET_HEADS * RET_DV
MLA_COLS = MLA_Q_LORA + MLA_KV_LORA + MLA_ROPE

LANES = 128
HALF = LANES // 2
N_PAIRS = 4
CONV_PAD = 8
DT_LANES = SSM_HEADS
KROPE_LO = MLA_NOPE
V7X_VMEM_LIMIT_BYTES = 56 * 1024 * 1024

F32 = jnp.float32
BF16 = jnp.bfloat16


def _bf(x):
    return x.astype(BF16)


def _dot(a, b):
    return jnp.dot(a, b, preferred_element_type=F32)


def _dot_nt(a, b):
    return lax.dot_general(a, b, (((1,), (1,)), ((), ())), preferred_element_type=F32)


def _sigmoid(x):
    return 1.0 / (1.0 + jnp.exp(-x))


def _silu(x):
    return x * _sigmoid(x)


def _rms(x):
    return x * lax.rsqrt(jnp.mean(x * x, axis=-1, keepdims=True) + EPS)


def _split3(x):
    hi = _bf(x)
    r = x - hi.astype(F32)
    mid = _bf(r)
    lo = _bf(r - mid.astype(F32))
    return hi, mid, lo


def _exact01(m01, x):
    hi, mid, lo = _split3(x)
    return _dot(m01, hi) + _dot(m01, mid) + _dot(m01, lo)


def _exact01_nt(m01, x):
    hi, mid, lo = _split3(x)
    return _dot_nt(m01, hi) + _dot_nt(m01, mid) + _dot_nt(m01, lo)


def _exact01_tn(x, m01):
    hi, mid, lo = _split3(x)
    return _dot(hi, m01) + _dot(mid, m01) + _dot(lo, m01)


def _lane_lo(rows):
    return lax.broadcasted_iota(jnp.int32, (rows, LANES), 1) < HALF


def _rope_slab(x, cos, sin_a, sin_b, shift):
    return x * cos + pltpu.roll(x, LANES - shift, 1) * sin_a + pltpu.roll(x, shift, 1) * sin_b


def _const_spec(shape):
    nd = len(shape)
    return pl.BlockSpec(shape, lambda *_: (0,) * nd, pipeline_mode=pl.Buffered(1))


def _params(n_axes):
    return pltpu.CompilerParams(dimension_semantics=("arbitrary",) * n_axes,
                                vmem_limit_bytes=V7X_VMEM_LIMIT_BYTES)


def _ffn_body(*refs, n_pre, fc):
    x_ref = refs[0]
    pre = [(refs[1 + 2 * i], refs[2 + 2 * i]) for i in range(n_pre)]
    g_ref, wg_ref, wu_ref, wd_ref, o_ref = refs[1 + 2 * n_pre:]
    x = x_ref[...]
    for m_ref, w_ref in pre:
        x = x + _dot(m_ref[...], w_ref[...])
    xn = _bf(_rms(x) * g_ref[...])
    acc = None
    for c in range(D_FF // fc):
        gate = _dot(xn, wg_ref[:, c * fc:(c + 1) * fc])
        up = _dot(xn, wu_ref[:, c * fc:(c + 1) * fc])
        part = _dot(_bf(_silu(gate) * up), wd_ref[c * fc:(c + 1) * fc, :])
        acc = part if acc is None else acc + part
    o_ref[...] = x + 0.5 * acc


def _ffn_call(x, pre, gain, wg, wu, wd, *, tm, fc=256):
    n = x.shape[0]
    tm = min(tm, n)
    assert n % tm == 0 and D_FF % fc == 0
    args, specs = [x], [pl.BlockSpec((tm, D_MODEL), lambda i: (i, 0))]
    for m, w in pre:
        args += [m, w]
        specs += [pl.BlockSpec((tm, m.shape[1]), lambda i: (i, 0)), _const_spec(w.shape)]
    args += [gain, wg, wu, wd]
    specs += [_const_spec(gain.shape), _const_spec(wg.shape), _const_spec(wu.shape), _const_spec(wd.shape)]
    return pl.pallas_call(
        functools.partial(_ffn_body, n_pre=len(pre), fc=fc),
        grid=(n // tm,),
        in_specs=specs,
        out_specs=pl.BlockSpec((tm, D_MODEL), lambda i: (i, 0)),
        out_shape=jax.ShapeDtypeStruct((n, D_MODEL), F32),
        compiler_params=_params(1),
        name="ffn",
    )(*args)


def _kv_from_latent(ckv_n, krope_slab, wkv_ref, kgain, cos, sin_a, sin_b, k_ref, v_ref):
    kvp = _dot(_bf(ckv_n), wkv_ref[...])
    for h in range(MLA_HEADS):
        kh = kvp[:, h * LANES:(h + 1) * LANES] + krope_slab
        ss = jnp.sum(kh * kh, axis=-1, keepdims=True) * (1.0 / MLA_DQK)
        kn = kh * lax.rsqrt(ss + EPS) * kgain
        k_ref[:, h * LANES:(h + 1) * LANES] = _bf(_rope_slab(kn, cos, sin_a, sin_b, MLA_ROPE // 2))
    v_ref[...] = _bf(kvp[:, MLA_HEADS * LANES:])


def _kvprep_body(ckv_ref, kr_ref, place_ref, wkv_ref, kgain_ref, tab_ref, k_ref, v_ref):
    krope_slab = _exact01_tn(kr_ref[...], place_ref[...])
    tab = tab_ref[...]
    _kv_from_latent(ckv_ref[...], krope_slab, wkv_ref, kgain_ref[...],
                    tab[:, 0:LANES], tab[:, LANES:2 * LANES], tab[:, 2 * LANES:3 * LANES], k_ref, v_ref)


def _kvprep_call(ckv, krope, wkv, kgain, tab, *, tm):
    b, s, _ = ckv.shape
    tm = min(tm, s)
    assert s % tm == 0
    place = np.zeros((MLA_ROPE, LANES), np.float32)
    place[np.arange(MLA_ROPE), KROPE_LO + np.arange(MLA_ROPE)] = 1.0
    place = jnp.asarray(place, BF16)
    return pl.pallas_call(
        _kvprep_body,
        grid=(b, s // tm),
        in_specs=[pl.BlockSpec((None, tm, MLA_KV_LORA), lambda i, j: (i, j, 0)),
                  pl.BlockSpec((None, tm, MLA_ROPE), lambda i, j: (i, j, 0)),
                  _const_spec(place.shape), _const_spec(wkv.shape), _const_spec(kgain.shape),
                  pl.BlockSpec((tm, 3 * LANES), lambda i, j: (j, 0))],
        out_specs=[pl.BlockSpec((None, tm, MLA_HEADS * LANES), lambda i, j: (i, j, 0)),
                   pl.BlockSpec((None, tm, MLA_HEADS * MLA_DV), lambda i, j: (i, j, 0))],
        out_shape=[jax.ShapeDtypeStruct((b, s, MLA_HEADS * LANES), BF16),
                   jax.ShapeDtypeStruct((b, s, MLA_HEADS * MLA_DV), BF16)],
        compiler_params=_params(2),
        name="kvprep",
    )(ckv, krope, place, wkv, kgain, tab)


def _inproj_body(x_ref, tab_ref, gmix_ref, wret_ref, wcq_ref, wckv_ref, wz_ref, wxbc_ref, wsm_ref,
                 gq_ref, gkv_ref, wuq_ref, qgain_ref, wkv_ref, kgain_ref,
                 rq_ref, rk_ref, rv_ref, rg_ref, mq_ref, ckv_ref, sm_ref, z_ref, xbc_ref, k_ref, v_ref):
    hn = _bf(_rms(x_ref[...]) * gmix_ref[...])
    tab = tab_ref[...]
    cos_r, sa_r, sb_r = tab[:, 0:LANES], tab[:, LANES:2 * LANES], tab[:, 2 * LANES:3 * LANES]
    cos_m, sa_m, sb_m = tab[:, 3 * LANES:4 * LANES], tab[:, 4 * LANES:5 * LANES], tab[:, 5 * LANES:6 * LANES]

    hk = RET_HEADS * RET_DK
    ret = _dot(hn, wret_ref[...])
    for j in range(N_PAIRS):
        sl = slice(j * LANES, (j + 1) * LANES)
        rq_ref[:, sl] = _bf(_rope_slab(ret[:, sl], cos_r, sa_r, sb_r, RET_DK // 2))
        kr = _rope_slab(ret[:, hk + j * LANES:hk + (j + 1) * LANES], cos_r, sa_r, sb_r, RET_DK // 2)
        rk_ref[:, sl] = _bf(kr * (RET_DK ** -0.5))
    rv_ref[...] = _bf(ret[:, 2 * hk:3 * hk])
    rg_ref[...] = _silu(ret[:, 3 * hk:])

    cq = _bf(_rms(_dot(hn, wcq_ref[...])) * gq_ref[...])
    qp = _dot(cq, wuq_ref[...])
    qgain = qgain_ref[...]
    for h in range(MLA_HEADS):
        qh = qp[:, h * LANES:(h + 1) * LANES]
        ss = jnp.sum(qh * qh, axis=-1, keepdims=True) * (1.0 / MLA_DQK)
        qn = qh * lax.rsqrt(ss + EPS) * qgain
        qr = _rope_slab(qn, cos_m, sa_m, sb_m, MLA_ROPE // 2)
        mq_ref[:, h * LANES:(h + 1) * LANES] = _bf(qr * (MLA_DQK ** -0.5))

    small = _dot(hn, wsm_ref[...])
    sm_ref[...] = small
    ckv_n = _rms(_dot(hn, wckv_ref[...])) * gkv_ref[...]
    ckv_ref[...] = ckv_n
    lane = lax.broadcasted_iota(jnp.int32, small.shape, 1)
    krope_slab = jnp.where((lane >= KROPE_LO) & (lane < KROPE_LO + MLA_ROPE), small, 0.0)
    _kv_from_latent(ckv_n, krope_slab, wkv_ref, kgain_ref[...], cos_m, sa_m, sb_m, k_ref, v_ref)

    z_ref[...] = _dot(hn, wz_ref[...])
    xbc_ref[...] = _dot(hn, wxbc_ref[...])


def _inproj_call(x, tab, w, *, tm, t_per_batch):
    n = x.shape[0]
    tm = min(tm, n)
    assert n % tm == 0
    if tm <= t_per_batch:
        assert t_per_batch % tm == 0 and tab.shape[0] == t_per_batch
        per = t_per_batch // tm
        tab_spec = pl.BlockSpec((tm, 6 * LANES), lambda i: (i % per, 0))
    else:
        assert tab.shape[0] == tm
        tab_spec = pl.BlockSpec((tm, 6 * LANES), lambda i: (0, 0))
    consts = [w["mix_norm"], w["w_ret"], w["w_cq"], w["w_ckv"], w["w_z"], w["w_xbc"], w["w_small"],
              w["q_norm"], w["kv_norm"], w["w_uq"], w["q_gain"], w["w_kv"], w["k_gain"]]
    widths = [(512, BF16), (512, BF16), (512, BF16), (512, F32), (MLA_HEADS * LANES, BF16),
              (MLA_KV_LORA, F32), (LANES, F32), (SSM_D_INNER, F32), (CONV_DIM, F32),
              (MLA_HEADS * LANES, BF16), (MLA_HEADS * MLA_DV, BF16)]
    return pl.pallas_call(
        _inproj_body,
        grid=(n // tm,),
        in_specs=[pl.BlockSpec((tm, D_MODEL), lambda i: (i, 0)), tab_spec] + [_const_spec(c.shape) for c in consts],
        out_specs=[pl.BlockSpec((tm, wd), lambda i: (i, 0)) for wd, _ in widths],
        out_shape=[jax.ShapeDtypeStruct((n, wd), dt) for wd, dt in widths],
        compiler_params=_params(1),
        name="inproj",
    )(x, tab, *consts)


def _ret_body(*refs, has_s0):
    (q_ref, k_ref, v_ref, g_ref, dm_ref, qd_ref, kd_ref, cd_ref, bd_ref, eye_ref, gn_ref) = refs[:11]
    s0_ref = refs[11] if has_s0 else None
    o_ref, s_ref = refs[-2:]
    rows = q_ref.shape[0]

    @pl.when(pl.program_id(1) == 0)
    def _():
        s_ref[...] = s0_ref[...] if has_s0 else jnp.zeros(s_ref.shape, F32)

    lo = _lane_lo(rows)
    m_lo = _bf(jnp.where(lo, 1.0, 0.0))
    m_hi = _bf(jnp.where(lo, 0.0, 1.0))
    eye = eye_ref[...]
    for j in range(N_PAIRS):
        sl = slice(j * LANES, (j + 1) * LANES)
        qp, kp, vp = q_ref[:, sl], k_ref[:, sl], v_ref[:, sl]
        state = s_ref[j]
        o = _dot(qp, _bf(state)) * qd_ref[:, sl]
        for e, m in enumerate((m_lo, m_hi)):
            att = _dot_nt(qp * m, kp) * dm_ref[2 * j + e]
            o = o + _dot(_bf(att), vp * m)
        oo = o * o
        s_lo = jnp.sum(jnp.where(lo, oo, 0.0), axis=-1, keepdims=True)
        s_hi = jnp.sum(jnp.where(lo, 0.0, oo), axis=-1, keepdims=True)
        ss = jnp.where(lo, s_lo, s_hi) * (1.0 / RET_DV)
        o_ref[:, sl] = _bf(o * lax.rsqrt(ss + EPS) * gn_ref[:, sl] * g_ref[:, sl])
        kdec = _bf(kp.astype(F32) * kd_ref[:, sl])
        kdec_t = _bf(_dot_nt(eye, kdec))
        s_ref[j] = state * cd_ref[j] + _dot(kdec_t, vp) * bd_ref[...]


def _ret_tables(c):
    log_g = jnp.log(1.0 - 2.0 ** (-5.0 - jnp.arange(RET_HEADS, dtype=F32)))
    idx = jnp.arange(c, dtype=F32)
    diff = idx[:, None] - idx[None, :]
    dmask = jnp.where(diff >= 0, jnp.exp(log_g[:, None, None] * jnp.maximum(diff, 0.0)), 0.0)
    q_dec = jnp.exp(log_g[:, None] * (idx[None, :] + 1.0))
    k_dec = jnp.exp(log_g[:, None] * (c - 1.0 - idx[None, :]))
    c_dec = jnp.exp(log_g * c)
    qd = jnp.repeat(q_dec.T, RET_DK, axis=1)
    kd = jnp.repeat(k_dec.T, RET_DK, axis=1)
    cd = jnp.broadcast_to(jnp.repeat(c_dec, RET_DK).reshape(N_PAIRS, LANES, 1), (N_PAIRS, LANES, LANES))
    blk = np.arange(LANES) // HALF
    bd = jnp.asarray((blk[:, None] == blk[None, :]).astype(np.float32))
    return dmask, qd, kd, cd, bd


def _ret_call(rq, rk, rv, rg, gn, s0, *, c):
    b, t, w = rq.shape
    c = min(c, t)
    assert t % c == 0
    dmask, qd, kd, cd, bd = _ret_tables(c)
    eye = jnp.eye(LANES, dtype=BF16)
    tok = lambda i, j: (i, j, 0)
    args = [rq, rk, rv, rg, dmask, qd, kd, cd, bd, eye, gn]
    specs = [pl.BlockSpec((None, c, w), tok)] * 4 + [_const_spec(a.shape) for a in args[4:]]
    if s0 is not None:
        args.append(s0)
        specs.append(pl.BlockSpec((None, N_PAIRS, LANES, LANES), lambda i, j: (i, 0, 0, 0)))
    return pl.pallas_call(
        functools.partial(_ret_body, has_s0=s0 is not None),
        grid=(b, t // c),
        in_specs=specs,
        out_specs=[pl.BlockSpec((None, c, w), tok),
                   pl.BlockSpec((None, N_PAIRS, LANES, LANES), lambda i, j: (i, 0, 0, 0))],
        out_shape=[jax.ShapeDtypeStruct((b, t, w), BF16),
                   jax.ShapeDtypeStruct((b, N_PAIRS, LANES, LANES), F32)],
        compiler_params=_params(2),
        name="retention",
    )(*args)


def _ssd_body(*refs, has_h0):
    (z_ref, xbc_ref, sm_ref, cw_ref, cb_ref, dtb_ref, alog_ref, dsk_ref, ng_ref,
     tri_ref, eye_ref, conv0_ref) = refs[:12]
    h0_ref = refs[12] if has_h0 else None
    o_ref, h_ref, xbuf, st = refs[-4:]
    rows = z_ref.shape[0]
    ci = pl.program_id(1)

    @pl.when(ci == 0)
    def _():
        xbuf[0:CONV_PAD, :] = conv0_ref[...]
        for g in range(SSM_GROUPS):
            st[g] = h0_ref[g].T if has_h0 else jnp.zeros(st.shape[1:], F32)

    xbuf[CONV_PAD:CONV_PAD + rows, :] = xbc_ref[...]
    conv = cb_ref[...]
    for wi in range(CONV_W):
        off = CONV_PAD - (CONV_W - 1) + wi
        conv = conv + cw_ref[wi:wi + 1, :] * xbuf[off:off + rows, :]
    xbuf[0:CONV_PAD, :] = xbuf[rows:rows + CONV_PAD, :]
    act = _silu(conv)
    xs = act[:, :SSM_D_INNER]

    lo = _lane_lo(rows)
    lane = lax.broadcasted_iota(jnp.int32, (rows, LANES), 1)
    lane1 = lax.broadcasted_iota(jnp.int32, (1, LANES), 1)
    dt_raw = jnp.where(lane < DT_LANES, sm_ref[...], 0.0) + dtb_ref[...]
    dtv = jnp.maximum(dt_raw, 0.0) + jnp.log1p(jnp.exp(-jnp.abs(dt_raw)))
    a_neg = jnp.where(lane1 < DT_LANES, -jnp.exp(alog_ref[...]), 0.0)
    dta = dtv * a_neg
    eye = eye_ref[...]
    a_cum = _exact01(tri_ref[...], dta)
    a_cum_t = _exact01_nt(eye, a_cum)
    exp_a = jnp.exp(a_cum)
    a_last = a_cum[rows - 1:rows, :]
    w_dec = jnp.exp(a_last - a_cum)
    e_last = jnp.exp(a_last)
    r_i = lax.broadcasted_iota(jnp.int32, (rows, rows), 0)
    c_i = lax.broadcasted_iota(jnp.int32, (rows, rows), 1)
    causal = r_i >= c_i
    lo1 = lane1 < HALF

    def pair_lanes(v, h):
        sel = lo if v.shape[0] == rows else lo1
        return jnp.where(sel, v[:, h:h + 1], v[:, h + 1:h + 2])

    for g in range(SSM_GROUPS):
        b_off = SSM_D_INNER + g * SSM_STATE
        c_off = SSM_D_INNER + SSM_GROUPS * SSM_STATE + g * SSM_STATE
        bmb = _bf(act[:, b_off:b_off + SSM_STATE])
        cmb = _bf(act[:, c_off:c_off + SSM_STATE])
        scores = _dot_nt(cmb, bmb)
        y_off = _dot(cmb, _bf(st[g]))
        ssq = jnp.zeros((rows, 1), F32)
        gated, xw, e_row = [], [], []
        for jj in range(N_PAIRS):
            j = g * N_PAIRS + jj
            sl = slice(j * LANES, (j + 1) * LANES)
            h0 = 2 * j
            xsl = xs[:, sl]
            xdt = xsl * pair_lanes(dtv, h0)
            y = y_off[:, jj * LANES:(jj + 1) * LANES] * pair_lanes(exp_a, h0) + dsk_ref[:, sl] * xsl
            for e in range(2):
                h = h0 + e
                seg = a_cum[:, h:h + 1] - a_cum_t[h:h + 1, :]
                dec = jnp.where(causal, jnp.exp(jnp.minimum(seg, 0.0)), 0.0)
                keep = lo if e == 0 else jnp.logical_not(lo)
                y = y + _dot(_bf(scores * dec), _bf(jnp.where(keep, xdt, 0.0)))
            zs = z_ref[:, sl]
            gt = y * _silu(zs)
            ssq = ssq + jnp.sum(gt * gt, axis=-1, keepdims=True)
            gated.append(gt)
            xw.append(_bf(xdt * pair_lanes(w_dec, h0)))
            e_row.append(pair_lanes(e_last, h0))
        rs = lax.rsqrt(ssq * (1.0 / (SSM_D_INNER // SSM_GROUPS)) + EPS)
        for jj in range(N_PAIRS):
            sl = slice((g * N_PAIRS + jj) * LANES, (g * N_PAIRS + jj + 1) * LANES)
            o_ref[:, sl] = _bf(gated[jj] * rs * ng_ref[:, sl])
        bm_t = _bf(_dot_nt(eye, bmb))
        upd = _dot(bm_t, jnp.concatenate(xw, axis=1))
        st[g] = st[g] * jnp.concatenate(e_row, axis=1) + upd

    @pl.when(ci == pl.num_programs(1) - 1)
    def _():
        for g in range(SSM_GROUPS):
            h_ref[g] = st[g].T


def _ssd_call(z, xbc, small, w, conv0, h0, *, c):
    b, t, _ = z.shape
    c = min(c, t)
    assert t % c == 0 and c >= CONV_PAD
    tri = jnp.asarray(np.tril(np.ones((c, c), np.float32)), BF16)
    eye = jnp.eye(LANES, dtype=BF16)
    gw = SSM_D_INNER // SSM_GROUPS
    tok = lambda i, j: (i, j, 0)
    consts = [w["conv_w"], w["conv_b"], w["dt_bias"], w["a_log"], w["d_skip"], w["ssm_norm"], tri, eye]
    args = [z, xbc, small] + consts + [conv0]
    specs = ([pl.BlockSpec((None, c, SSM_D_INNER), tok), pl.BlockSpec((None, c, CONV_DIM), tok),
              pl.BlockSpec((None, c, LANES), tok)] + [_const_spec(a.shape) for a in consts]
             + [pl.BlockSpec((None, CONV_PAD, CONV_DIM), lambda i, j: (i, 0, 0))])
    if h0 is not None:
        args.append(h0)
        specs.append(pl.BlockSpec((None, SSM_GROUPS, gw, SSM_STATE), lambda i, j: (i, 0, 0, 0)))
    return pl.pallas_call(
        functools.partial(_ssd_body, has_h0=h0 is not None),
        grid=(b, t // c),
        in_specs=specs,
        out_specs=[pl.BlockSpec((None, c, SSM_D_INNER), tok),
                   pl.BlockSpec((None, SSM_GROUPS, gw, SSM_STATE), lambda i, j: (i, 0, 0, 0))],
        out_shape=[jax.ShapeDtypeStruct((b, t, SSM_D_INNER), BF16),
                   jax.ShapeDtypeStruct((b, SSM_GROUPS, gw, SSM_STATE), F32)],
        scratch_shapes=[pltpu.VMEM((CONV_PAD + c, CONV_DIM), F32),
                        pltpu.VMEM((SSM_GROUPS, SSM_STATE, gw), F32)],
        compiler_params=_params(2),
        name="ssd",
    )(*args)


def _mla_prompt_body(q_ref, k_ref, v_ref, o_ref, m_sc, l_sc, acc_sc, *, tq):
    qi = pl.program_id(1)
    m_sc[...] = jnp.full(m_sc.shape, -jnp.inf, F32)
    l_sc[...] = jnp.zeros(l_sc.shape, F32)
    acc_sc[...] = jnp.zeros(acc_sc.shape, F32)
    lo = _lane_lo(tq)
    m_lo = _bf(jnp.where(lo, 1.0, 0.0))
    m_hi = _bf(jnp.where(lo, 0.0, 1.0))
    r_i = lax.broadcasted_iota(jnp.int32, (tq, tq), 0)
    c_i = lax.broadcasted_iota(jnp.int32, (tq, tq), 1)
    visible = lax.shift_right_logical(c_i, 6) <= lax.shift_right_logical(r_i, 6)

    def step(t, diagonal):
        off = pl.multiple_of(t * tq, tq)
        for j in range(N_PAIRS):
            sl = slice(j * LANES, (j + 1) * LANES)
            vblk = v_ref[pl.ds(off, tq), sl]
            pv, alphas = None, []
            for e, msk in enumerate((m_lo, m_hi)):
                h = 2 * j + e
                hs = slice(h * LANES, (h + 1) * LANES)
                s = _dot_nt(q_ref[:, hs], k_ref[pl.ds(off, tq), hs])
                if diagonal:
                    s = jnp.where(visible, s, -jnp.inf)
                m_old = m_sc[h]
                m_new = jnp.maximum(m_old, jnp.max(s, axis=-1, keepdims=True))
                alpha = jnp.exp(m_old - m_new)
                p = jnp.exp(s - m_new)
                l_sc[h] = alpha * l_sc[h] + jnp.sum(p, axis=-1, keepdims=True)
                m_sc[h] = m_new
                d = _dot(_bf(p), vblk * msk)
                pv = d if pv is None else pv + d
                alphas.append(alpha)
            acc_sc[:, sl] = acc_sc[:, sl] * jnp.where(lo, alphas[0], alphas[1]) + pv

    def body(t, carry):
        step(t, False)
        return carry

    lax.fori_loop(0, qi, body, 0)
    step(qi, True)
    for j in range(N_PAIRS):
        sl = slice(j * LANES, (j + 1) * LANES)
        o_ref[:, sl] = _bf(acc_sc[:, sl] / jnp.where(lo, l_sc[2 * j], l_sc[2 * j + 1]))


def _mla_prompt_call(q, k, v, *, tq):
    b, t, _ = q.shape
    tq = min(tq, t)
    assert t % tq == 0 and tq % CHUNK == 0 and CHUNK == 64
    return pl.pallas_call(
        functools.partial(_mla_prompt_body, tq=tq),
        grid=(b, t // tq),
        in_specs=[pl.BlockSpec((None, tq, MLA_HEADS * LANES), lambda i, j: (i, j, 0)),
                  pl.BlockSpec((None, t, MLA_HEADS * LANES), lambda i, j: (i, 0, 0)),
                  pl.BlockSpec((None, t, MLA_HEADS * MLA_DV), lambda i, j: (i, 0, 0))],
        out_specs=pl.BlockSpec((None, tq, MLA_HEADS * MLA_DV), lambda i, j: (i, j, 0)),
        out_shape=jax.ShapeDtypeStruct((b, t, MLA_HEADS * MLA_DV), BF16),
        scratch_shapes=[pltpu.VMEM((MLA_HEADS, tq, 1), F32), pltpu.VMEM((MLA_HEADS, tq, 1), F32),
                        pltpu.VMEM((tq, MLA_HEADS * MLA_DV), F32)],
        compiler_params=_params(2),
        name="mla_prompt",
    )(q, k, v)


def _mla_sample_body(q_ref, kp_ref, vp_ref, kn_ref, vn_ref, visp_ref, visn_ref, o_ref):
    rows = q_ref.shape[0]
    lo = _lane_lo(rows)
    m_lo = _bf(jnp.where(lo[:1], 1.0, 0.0))
    m_hi = _bf(jnp.where(lo[:1], 0.0, 1.0))
    vis_p = visp_ref[...] > 0.5
    vis_n = visn_ref[...] > 0.5
    for j in range(N_PAIRS):
        sl = slice(j * LANES, (j + 1) * LANES)
        vp, vn = vp_ref[:, sl], vn_ref[:, sl]
        o = None
        for e, msk in enumerate((m_lo, m_hi)):
            hs = slice((2 * j + e) * LANES, (2 * j + e + 1) * LANES)
            qh = q_ref[:, hs]
            s_p = jnp.where(vis_p, _dot_nt(qh, kp_ref[:, hs]), -jnp.inf)
            s_n = jnp.where(vis_n, _dot_nt(qh, kn_ref[:, hs]), -jnp.inf)
            m = jnp.maximum(jnp.max(s_p, axis=-1, keepdims=True), jnp.max(s_n, axis=-1, keepdims=True))
            p_p, p_n = jnp.exp(s_p - m), jnp.exp(s_n - m)
            inv = 1.0 / (jnp.sum(p_p, axis=-1, keepdims=True) + jnp.sum(p_n, axis=-1, keepdims=True))
            d = _dot(_bf(p_p * inv), vp * msk) + _dot(_bf(p_n * inv), vn * msk)
            o = d if o is None else o + d
        o_ref[:, sl] = _bf(o)


def _mla_sample_call(q, k_past, v_past, k_new, v_new, q_pos, past_pos):
    b, t, _ = q.shape
    s = k_past.shape[1]
    vis_p = jnp.asarray((past_pos[None, :] // CHUNK <= q_pos[:, None] // CHUNK).astype(np.float32))
    vis_n = jnp.asarray((q_pos[None, :] // CHUNK <= q_pos[:, None] // CHUNK).astype(np.float32))
    kw, vw = MLA_HEADS * LANES, MLA_HEADS * MLA_DV
    bat = lambda i: (i, 0, 0)
    return pl.pallas_call(
        _mla_sample_body,
        grid=(b,),
        in_specs=[pl.BlockSpec((None, t, kw), bat), pl.BlockSpec((None, s, kw), bat),
                  pl.BlockSpec((None, s, vw), bat), pl.BlockSpec((None, t, kw), bat),
                  pl.BlockSpec((None, t, vw), bat), _const_spec(vis_p.shape), _const_spec(vis_n.shape)],
        out_specs=pl.BlockSpec((None, t, vw), bat),
        out_shape=jax.ShapeDtypeStruct((b, t, vw), BF16),
        compiler_params=_params(1),
        name="mla_sample",
    )(q, k_past, v_past, k_new, v_new, vis_p, vis_n)


def _rope_tables(pos):
    posf = pos.astype(F32)
    th_r = 1.0 / (10000.0 ** jnp.linspace(0.0, 1.0, RET_DK // 2, dtype=F32))
    ang = posf[:, None] * th_r[None, :]
    c, s = jnp.cos(ang), jnp.sin(ang)
    zr = jnp.zeros_like(s)
    ret = [jnp.concatenate([c, c, c, c], 1), jnp.concatenate([-s, zr, -s, zr], 1), jnp.concatenate([zr, s, zr, s], 1)]
    th_m = 1.0 / (ROPE_BASE ** (jnp.arange(0, MLA_ROPE, 2, dtype=F32) / MLA_ROPE))
    angm = posf[:, None] * th_m[None, :]
    cm, sm = jnp.cos(angm), jnp.sin(angm)
    n = pos.shape[0]
    one, z64 = jnp.ones((n, MLA_NOPE), F32), jnp.zeros((n, MLA_NOPE), F32)
    z16, z32 = jnp.zeros_like(sm), jnp.zeros((n, LANES - MLA_DQK), F32)
    mla = [jnp.concatenate([one, cm, cm, z32], 1), jnp.concatenate([z64, -sm, z16, z32], 1),
           jnp.concatenate([z64, z16, sm, z32], 1)]
    return jnp.concatenate(ret + mla, axis=1)


def _layer_weights(l, p):
    w_in = p["w_in"][l]
    o = RET_COLS
    w_cq, w_ckv = w_in[:, o:o + MLA_Q_LORA], w_in[:, o + MLA_Q_LORA:o + MLA_Q_LORA + MLA_KV_LORA]
    w_kr = w_in[:, o + MLA_Q_LORA + MLA_KV_LORA:o + MLA_COLS]
    o += MLA_COLS
    w_z, w_xbc, w_dt = w_in[:, o:o + SSM_D_INNER], w_in[:, o + SSM_D_INNER:o + SSM_D_INNER + CONV_DIM], w_in[:, o + SSM_D_INNER + CONV_DIM:]
    zc = lambda n: jnp.zeros((D_MODEL, n), F32)
    w_small = jnp.concatenate([w_dt, zc(KROPE_LO - DT_LANES), w_kr, zc(LANES - KROPE_LO - MLA_ROPE)], axis=1)
    w_uq = jnp.pad(p["mla_w_uq"][l].reshape(MLA_Q_LORA, MLA_HEADS, MLA_DQK), ((0, 0), (0, 0), (0, LANES - MLA_DQK)))
    w_ukv = p["mla_w_ukv"][l].reshape(MLA_KV_LORA, MLA_HEADS, MLA_NOPE + MLA_DV)
    w_kn = jnp.pad(w_ukv[:, :, :MLA_NOPE], ((0, 0), (0, 0), (0, LANES - MLA_NOPE)))
    w_kv = jnp.concatenate([w_kn.reshape(MLA_KV_LORA, -1), w_ukv[:, :, MLA_NOPE:].reshape(MLA_KV_LORA, -1)], axis=1)
    pad_gain = lambda g: jnp.pad(g, (0, LANES - MLA_DQK)).reshape(1, LANES)
    pad16 = lambda v: jnp.pad(v, (0, LANES - DT_LANES)).reshape(1, LANES)
    w_out = p["w_out"][l]
    d_ret, d_mla = RET_HEADS * RET_DV, MLA_HEADS * MLA_DV
    out = {
        "mix_norm": p["mix_norm"][l].reshape(1, -1), "w_ret": _bf(w_in[:, :RET_COLS]), "w_cq": _bf(w_cq),
        "w_ckv": _bf(w_ckv), "w_z": _bf(w_z), "w_xbc": _bf(w_xbc), "w_small": _bf(w_small),
        "q_norm": p["mla_q_norm"][l].reshape(1, -1), "kv_norm": p["mla_kv_norm"][l].reshape(1, -1),
        "w_uq": _bf(w_uq.reshape(MLA_Q_LORA, -1)), "q_gain": pad_gain(p["mla_q_gain"][l]),
        "w_kv": _bf(w_kv), "k_gain": pad_gain(p["mla_k_gain"][l]),
        "ret_norm": p["ret_norm"][l].reshape(1, -1),
        "conv_w": p["ssm_conv_w"][l], "conv_b": p["ssm_conv_b"][l].reshape(1, -1),
        "dt_bias": pad16(p["ssm_dt_bias"][l]), "a_log": pad16(p["ssm_a_log"][l]),
        "d_skip": jnp.repeat(p["ssm_d"][l], SSM_HEADDIM).reshape(1, -1), "ssm_norm": p["ssm_norm"][l].reshape(1, -1),
        "wo_ret": _bf(w_out[:d_ret]), "wo_mla": _bf(w_out[d_ret:d_ret + d_mla]), "wo_ssm": _bf(w_out[d_ret + d_mla:]),
    }
    for tag in ("ffn1", "ffn2"):
        out[tag + "_norm"] = p[tag + "_norm"][l].reshape(1, -1)
        out[tag + "_wg"] = _bf(p[tag + "_wgu"][l][:, :D_FF])
        out[tag + "_wu"] = _bf(p[tag + "_wgu"][l][:, D_FF:])
        out[tag + "_wd"] = _bf(p[tag + "_wd"][l])
    return out


def _pack_ret_state(s):
    b = s.shape[0]
    s = s.reshape(b, N_PAIRS, 2, RET_DK, RET_DV)
    z = jnp.zeros_like(s[:, :, 0])
    top = jnp.concatenate([s[:, :, 0], z], axis=-1)
    bot = jnp.concatenate([z, s[:, :, 1]], axis=-1)
    return jnp.concatenate([top, bot], axis=-2)


def _unpack_ret_state(sp):
    b = sp.shape[0]
    return jnp.stack([sp[:, :, :RET_DK, :RET_DV], sp[:, :, RET_DK:, RET_DV:]], axis=2).reshape(b, RET_HEADS, RET_DK, RET_DV)


def _layer(x, tab, w, *, ret_s0, ssm_h0, conv_buf, past, cfg):
    b, t, _ = x.shape
    n = b * t
    x1 = _ffn_call(x.reshape(n, D_MODEL), [], w["ffn1_norm"], w["ffn1_wg"], w["ffn1_wu"], w["ffn1_wd"], tm=cfg["tm_ffn"])
    rq, rk, rv, rg, mq, ckv, small, z, xbc, kk, vv = _inproj_call(x1, tab, w, tm=cfg["tm_in"], t_per_batch=t)
    r3 = lambda a: a.reshape(b, t, a.shape[-1])
    o_ret, s_ret = _ret_call(r3(rq), r3(rk), r3(rv), r3(rg), w["ret_norm"],
                             None if ret_s0 is None else _pack_ret_state(ret_s0), c=cfg["c_ret"])
    if conv_buf is None:
        conv0 = jnp.zeros((b, CONV_PAD, CONV_DIM), F32)
    else:
        conv0 = jnp.concatenate([jnp.zeros((b, CONV_PAD - (CONV_W - 1), CONV_DIM), F32), conv_buf], axis=1)
    h0 = None if ssm_h0 is None else ssm_h0.reshape(b, SSM_GROUPS, -1, SSM_STATE)
    o_ssm, h_ssm = _ssd_call(r3(z), r3(xbc), r3(small), w, conv0, h0, c=cfg["c_ssd"])
    if past is None:
        o_mla = _mla_prompt_call(r3(mq), r3(kk), r3(vv), tq=cfg["tq"])
    else:
        k_past, v_past, q_pos, past_pos = past
        o_mla = _mla_sample_call(r3(mq), k_past, v_past, r3(kk), r3(vv), q_pos, past_pos)
    f2 = lambda a: a.reshape(n, a.shape[-1])
    x3 = _ffn_call(x1, [(f2(o_ret), w["wo_ret"]), (f2(o_mla), w["wo_mla"]), (f2(o_ssm), w["wo_ssm"])],
                   w["ffn2_norm"], w["ffn2_wg"], w["ffn2_wu"], w["ffn2_wd"], tm=cfg["tm_ffn"])
    xbc3 = r3(xbc)
    if conv_buf is None:
        assert t >= CONV_W - 1
        conv_new = xbc3[:, t - (CONV_W - 1):]
    else:
        conv_new = jnp.concatenate([conv_buf, xbc3], axis=1)[:, t:]
    state = (r3(ckv), r3(small)[:, :, KROPE_LO:KROPE_LO + MLA_ROPE], _unpack_ret_state(s_ret),
             h_ssm.reshape(b, SSM_HEADS, SSM_HEADDIM, SSM_STATE), conv_new)
    return x3.reshape(b, t, D_MODEL), state


PROMPT_CFG = dict(tm_ffn=512, tm_in=256, c_ret=128, c_ssd=128, tq=256)
SAMPLE_CFG = dict(tm_ffn=256, tm_in=256, c_ret=128, c_ssd=128, tq=256)


def kernel(x_prompt, x_sample, cache_mla_ckv, cache_mla_krope, state_ret, state_ssm, state_conv, ffn1_norm, ffn1_wgu, ffn1_wd, mix_norm, w_in, ret_norm, mla_q_norm, mla_kv_norm, mla_w_uq, mla_w_ukv, mla_q_gain, mla_k_gain, ssm_conv_w, ssm_conv_b, ssm_dt_bias, ssm_a_log, ssm_d, ssm_norm, w_out, ffn2_norm, ffn2_wgu, ffn2_wd):
    p = dict(ffn1_norm=ffn1_norm, ffn1_wgu=ffn1_wgu, ffn1_wd=ffn1_wd, mix_norm=mix_norm, w_in=w_in, ret_norm=ret_norm,
             mla_q_norm=mla_q_norm, mla_kv_norm=mla_kv_norm, mla_w_uq=mla_w_uq, mla_w_ukv=mla_w_ukv,
             mla_q_gain=mla_q_gain, mla_k_gain=mla_k_gain, ssm_conv_w=ssm_conv_w, ssm_conv_b=ssm_conv_b,
             ssm_dt_bias=ssm_dt_bias, ssm_a_log=ssm_a_log, ssm_d=ssm_d, ssm_norm=ssm_norm, w_out=w_out,
             ffn2_norm=ffn2_norm, ffn2_wgu=ffn2_wgu, ffn2_wd=ffn2_wd)
    depth = ffn1_norm.shape[0]
    b_s, t_s, _ = x_sample.shape
    t_p = x_prompt.shape[1]
    past = cache_mla_ckv.shape[2]
    q_pos_s = past + np.arange(t_s)
    past_pos = np.arange(past)
    tab_p = _rope_tables(jnp.arange(t_p, dtype=jnp.int32))
    tab_s = jnp.tile(_rope_tables(jnp.asarray(q_pos_s, jnp.int32)), (b_s, 1))
    tab_past = _rope_tables(jnp.arange(past, dtype=jnp.int32))[:, 3 * LANES:]
    yp, ys = x_prompt, x_sample
    new_p, new_s = [[] for _ in range(5)], [[] for _ in range(5)]
    for l in range(depth):
        w = _layer_weights(l, p)
        yp, st_p = _layer(yp, tab_p, w, ret_s0=None, ssm_h0=None, conv_buf=None, past=None, cfg=PROMPT_CFG)
        k_past, v_past = _kvprep_call(cache_mla_ckv[l], cache_mla_krope[l], w["w_kv"], w["k_gain"], tab_past, tm=512)
        ys, st_s = _layer(ys, tab_s, w, ret_s0=state_ret[l], ssm_h0=state_ssm[l], conv_buf=state_conv[l],
                          past=(k_past, v_past, q_pos_s, past_pos), cfg=SAMPLE_CFG)
        for i in range(5):
            new_p[i].append(st_p[i])
            new_s[i].append(st_s[i])
    return (yp, ys) + tuple(jnp.stack(a) for a in new_p) + tuple(jnp.stack(a) for a in new_s)
```

```python
import functools

import numpy as np
import jax
import jax.numpy as jnp
from jax import lax
from jax.experimental import pallas as pl
from jax.experimental.pallas import tpu as pltpu

D_MODEL = 1024
D_FF = 2816
CHUNK = 64
EPS = 1e-6
RET_HEADS = 8
RET_DK = 64
RET_DV = 64
MLA_HEADS = 8
MLA_Q_LORA = 384
MLA_KV_LORA = 256
MLA_NOPE = 64
MLA_ROPE = 32
MLA_DQK = MLA_NOPE + MLA_ROPE
MLA_DV = 64
ROPE_BASE = 10000.0
SSM_HEADS = 16
SSM_HEADDIM = 64
SSM_D_INNER = SSM_HEADS * SSM_HEADDIM
SSM_GROUPS = 2
SSM_STATE = 128
CONV_W = 4
CONV_DIM = SSM_D_INNER + 2 * SSM_GROUPS * SSM_STATE
RET_COLS = 2 * RET_HEADS * RET_DK + 2 * RET_HEADS * RET_DV
MLA_COLS = MLA_Q_LORA + MLA_KV_LORA + MLA_ROPE

LANES = 128
HALF = LANES // 2
N_PAIRS = 4
CONV_PAD = 8
DT_LANES = SSM_HEADS
KROPE_LO = MLA_NOPE
V7X_VMEM_LIMIT_BYTES = 56 * 1024 * 1024

F32 = jnp.float32
BF16 = jnp.bfloat16


def _bf(x):
    return x.astype(BF16)


def _dot(a, b):
    return jnp.dot(a, b, preferred_element_type=F32)


def _dot_nt(a, b):
    return lax.dot_general(a, b, (((1,), (1,)), ((), ())), preferred_element_type=F32)


def _sigmoid(x):
    return 1.0 / (1.0 + jnp.exp(-x))


def _silu(x):
    return x * _sigmoid(x)


def _rms(x):
    return x * lax.rsqrt(jnp.mean(x * x, axis=-1, keepdims=True) + EPS)


def _split3(x):
    hi = _bf(x)
    r = x - hi.astype(F32)
    mid = _bf(r)
    lo = _bf(r - mid.astype(F32))
    return hi, mid, lo


def _exact01(m01, x):
    hi, mid, lo = _split3(x)
    return _dot(m01, hi) + _dot(m01, mid) + _dot(m01, lo)


def _exact01_nt(m01, x):
    hi, mid, lo = _split3(x)
    return _dot_nt(m01, hi) + _dot_nt(m01, mid) + _dot_nt(m01, lo)


def _exact01_tn(x, m01):
    hi, mid, lo = _split3(x)
    return _dot(hi, m01) + _dot(mid, m01) + _dot(lo, m01)


def _lane_lo(rows):
    return lax.broadcasted_iota(jnp.int32, (rows, LANES), 1) < HALF


def _rope_slab(x, cos, sin_a, sin_b, shift):
    return x * cos + pltpu.roll(x, LANES - shift, 1) * sin_a + pltpu.roll(x, shift, 1) * sin_b


def _const_spec(shape):
    nd = len(shape)
    return pl.BlockSpec(shape, lambda *_: (0,) * nd, pipeline_mode=pl.Buffered(1))


def _params(n_axes):
    return pltpu.CompilerParams(dimension_semantics=("arbitrary",) * n_axes,
                                vmem_limit_bytes=V7X_VMEM_LIMIT_BYTES)


def _ffn_body(*refs, n_pre, fc):
    x_ref = refs[0]
    pre = [(refs[1 + 2 * i], refs[2 + 2 * i]) for i in range(n_pre)]
    g_ref, wg_ref, wu_ref, wd_ref, o_ref = refs[1 + 2 * n_pre:]
    x = x_ref[...]
    for m_ref, w_ref in pre:
        x = x + _dot(m_ref[...], w_ref[...])
    xn = _bf(_rms(x) * g_ref[...])
    acc = None
    for c in range(D_FF // fc):
        gate = _dot(xn, wg_ref[:, c * fc:(c + 1) * fc])
        up = _dot(xn, wu_ref[:, c * fc:(c + 1) * fc])
        part = _dot(_bf(_silu(gate) * up), wd_ref[c * fc:(c + 1) * fc, :])
        acc = part if acc is None else acc + part
    o_ref[...] = x + 0.5 * acc


def _ffn_call(x, pre, gain, wg, wu, wd, *, tm, fc=256):
    n = x.shape[0]
    tm = min(tm, n)
    assert n % tm == 0 and D_FF % fc == 0
    args, specs = [x], [pl.BlockSpec((tm, D_MODEL), lambda i: (i, 0))]
    for m, w in pre:
        args += [m, w]
        specs += [pl.BlockSpec((tm, m.shape[1]), lambda i: (i, 0)), _const_spec(w.shape)]
    args += [gain, wg, wu, wd]
    specs += [_const_spec(gain.shape), _const_spec(wg.shape), _const_spec(wu.shape), _const_spec(wd.shape)]
    return pl.pallas_call(
        functools.partial(_ffn_body, n_pre=len(pre), fc=fc),
        grid=(n // tm,),
        in_specs=specs,
        out_specs=pl.BlockSpec((tm, D_MODEL), lambda i: (i, 0)),
        out_shape=jax.ShapeDtypeStruct((n, D_MODEL), F32),
        compiler_params=_params(1),
        name="ffn",
    )(*args)


def _kv_from_latent(ckv_n, krope_slab, wkv_ref, kgain, cos, sin_a, sin_b, k_ref, v_ref):
    kvp = _dot(_bf(ckv_n), wkv_ref[...])
    ones_hi = jnp.where(_lane_lo(kvp.shape[0]), 0.0, 1.0)
    for h in range(MLA_HEADS):
        kh = kvp[:, h * LANES:(h + 1) * LANES] + krope_slab
        ss = jnp.sum(kh * kh, axis=-1, keepdims=True) * (1.0 / MLA_DQK)
        kn = kh * lax.rsqrt(ss + EPS) * kgain
        k_ref[:, h * LANES:(h + 1) * LANES] = _bf(_rope_slab(kn, cos, sin_a, sin_b, MLA_ROPE // 2))
        vs = slice((MLA_HEADS + h) * LANES, (MLA_HEADS + h + 1) * LANES)
        v_ref[:, h * LANES:(h + 1) * LANES] = _bf(kvp[:, vs] + ones_hi)


def _kvprep_body(ckv_ref, kr_ref, place_ref, wkv_ref, kgain_ref, tab_ref, k_ref, v_ref):
    krope_slab = _exact01_tn(kr_ref[...], place_ref[...])
    tab = tab_ref[...]
    _kv_from_latent(ckv_ref[...], krope_slab, wkv_ref, kgain_ref[...],
                    tab[:, 0:LANES], tab[:, LANES:2 * LANES], tab[:, 2 * LANES:3 * LANES], k_ref, v_ref)


def _kvprep_call(ckv, krope, wkv, kgain, tab, *, tm):
    b, s, _ = ckv.shape
    tm = min(tm, s)
    assert s % tm == 0
    place = np.zeros((MLA_ROPE, LANES), np.float32)
    place[np.arange(MLA_ROPE), KROPE_LO + np.arange(MLA_ROPE)] = 1.0
    place = jnp.asarray(place, BF16)
    return pl.pallas_call(
        _kvprep_body,
        grid=(b, s // tm),
        in_specs=[pl.BlockSpec((None, tm, MLA_KV_LORA), lambda i, j: (i, j, 0)),
                  pl.BlockSpec((None, tm, MLA_ROPE), lambda i, j: (i, j, 0)),
                  _const_spec(place.shape), _const_spec(wkv.shape), _const_spec(kgain.shape),
                  pl.BlockSpec((tm, 3 * LANES), lambda i, j: (j, 0))],
        out_specs=[pl.BlockSpec((None, tm, MLA_HEADS * LANES), lambda i, j: (i, j, 0))] * 2,
        out_shape=[jax.ShapeDtypeStruct((b, s, MLA_HEADS * LANES), BF16)] * 2,
        compiler_params=_params(2),
        name="kvprep",
    )(ckv, krope, place, wkv, kgain, tab)


def _inproj_body(x_ref, tab_ref, gmix_ref, wret_ref, wcq_ref, wckv_ref, wz_ref, wxbc_ref, wsm_ref,
                 gq_ref, gkv_ref, wuq_ref, qgain_ref, wkv_ref, kgain_ref,
                 rq_ref, rk_ref, rv_ref, rg_ref, mq_ref, ckv_ref, sm_ref, z_ref, xbc_ref, k_ref, v_ref):
    hn = _bf(_rms(x_ref[...]) * gmix_ref[...])
    tab = tab_ref[...]
    cos_r, sa_r, sb_r = tab[:, 0:LANES], tab[:, LANES:2 * LANES], tab[:, 2 * LANES:3 * LANES]
    cos_m, sa_m, sb_m = tab[:, 3 * LANES:4 * LANES], tab[:, 4 * LANES:5 * LANES], tab[:, 5 * LANES:6 * LANES]

    hk = RET_HEADS * RET_DK
    ret = _dot(hn, wret_ref[...])
    for j in range(N_PAIRS):
        sl = slice(j * LANES, (j + 1) * LANES)
        rq_ref[:, sl] = _bf(_rope_slab(ret[:, sl], cos_r, sa_r, sb_r, RET_DK // 2))
        kr = _rope_slab(ret[:, hk + j * LANES:hk + (j + 1) * LANES], cos_r, sa_r, sb_r, RET_DK // 2)
        rk_ref[:, sl] = _bf(kr * (RET_DK ** -0.5))
    rv_ref[...] = _bf(ret[:, 2 * hk:3 * hk])
    rg_ref[...] = _silu(ret[:, 3 * hk:])

    cq = _bf(_rms(_dot(hn, wcq_ref[...])) * gq_ref[...])
    qp = _dot(cq, wuq_ref[...])
    qgain = qgain_ref[...]
    for h in range(MLA_HEADS):
        qh = qp[:, h * LANES:(h + 1) * LANES]
        ss = jnp.sum(qh * qh, axis=-1, keepdims=True) * (1.0 / MLA_DQK)
        qn = qh * lax.rsqrt(ss + EPS) * qgain
        qr = _rope_slab(qn, cos_m, sa_m, sb_m, MLA_ROPE // 2)
        mq_ref[:, h * LANES:(h + 1) * LANES] = _bf(qr * (MLA_DQK ** -0.5))

    small = _dot(hn, wsm_ref[...])
    sm_ref[...] = small
    ckv_n = _rms(_dot(hn, wckv_ref[...])) * gkv_ref[...]
    ckv_ref[...] = ckv_n
    lane = lax.broadcasted_iota(jnp.int32, small.shape, 1)
    krope_slab = jnp.where((lane >= KROPE_LO) & (lane < KROPE_LO + MLA_ROPE), small, 0.0)
    _kv_from_latent(ckv_n, krope_slab, wkv_ref, kgain_ref[...], cos_m, sa_m, sb_m, k_ref, v_ref)

    z_ref[...] = _dot(hn, wz_ref[...])
    xbc_ref[...] = _dot(hn, wxbc_ref[...])


def _inproj_call(x, tab, w, *, tm, t_per_batch):
    n = x.shape[0]
    tm = min(tm, n)
    assert n % tm == 0
    if tm <= t_per_batch:
        assert t_per_batch % tm == 0 and tab.shape[0] == t_per_batch
        per = t_per_batch // tm
        tab_spec = pl.BlockSpec((tm, 6 * LANES), lambda i: (i % per, 0))
    else:
        assert tab.shape[0] == tm
        tab_spec = pl.BlockSpec((tm, 6 * LANES), lambda i: (0, 0))
    consts = [w["mix_norm"], w["w_ret"], w["w_cq"], w["w_ckv"], w["w_z"], w["w_xbc"], w["w_small"],
              w["q_norm"], w["kv_norm"], w["w_uq"], w["q_gain"], w["w_kv"], w["k_gain"]]
    widths = [(512, BF16), (512, BF16), (512, BF16), (512, F32), (MLA_HEADS * LANES, BF16),
              (MLA_KV_LORA, F32), (LANES, F32), (SSM_D_INNER, F32), (CONV_DIM, F32),
              (MLA_HEADS * LANES, BF16), (MLA_HEADS * LANES, BF16)]
    return pl.pallas_call(
        _inproj_body,
        grid=(n // tm,),
        in_specs=[pl.BlockSpec((tm, D_MODEL), lambda i: (i, 0)), tab_spec] + [_const_spec(c.shape) for c in consts],
        out_specs=[pl.BlockSpec((tm, wd), lambda i: (i, 0)) for wd, _ in widths],
        out_shape=[jax.ShapeDtypeStruct((n, wd), dt) for wd, dt in widths],
        compiler_params=_params(1),
        name="inproj",
    )(x, tab, *consts)


def _ret_body(*refs, has_s0):
    (q_ref, k_ref, v_ref, g_ref, dm_ref, qd_ref, kd_ref, cd_ref, bd_ref, eye_ref, gn_ref) = refs[:11]
    s0_ref = refs[11] if has_s0 else None
    o_ref, s_ref = refs[-2:]
    rows = q_ref.shape[0]

    @pl.when(pl.program_id(1) == 0)
    def _():
        s_ref[...] = s0_ref[...] if has_s0 else jnp.zeros(s_ref.shape, F32)

    lo = _lane_lo(rows)
    m_lo = _bf(jnp.where(lo, 1.0, 0.0))
    m_hi = _bf(jnp.where(lo, 0.0, 1.0))
    eye = eye_ref[...]
    for j in range(N_PAIRS):
        sl = slice(j * LANES, (j + 1) * LANES)
        qp, kp, vp = q_ref[:, sl], k_ref[:, sl], v_ref[:, sl]
        state = s_ref[j]
        o = _dot(qp, _bf(state)) * qd_ref[:, sl]
        for e, m in enumerate((m_lo, m_hi)):
            att = _dot_nt(qp * m, kp) * dm_ref[2 * j + e]
            o = o + _dot(_bf(att), vp * m)
        oo = o * o
        s_lo = jnp.sum(jnp.where(lo, oo, 0.0), axis=-1, keepdims=True)
        s_hi = jnp.sum(jnp.where(lo, 0.0, oo), axis=-1, keepdims=True)
        ss = jnp.where(lo, s_lo, s_hi) * (1.0 / RET_DV)
        o_ref[:, sl] = _bf(o * lax.rsqrt(ss + EPS) * gn_ref[:, sl] * g_ref[:, sl])
        kdec = _bf(kp.astype(F32) * kd_ref[:, sl])
        kdec_t = _bf(_dot_nt(eye, kdec))
        s_ref[j] = state * cd_ref[j] + _dot(kdec_t, vp) * bd_ref[...]


def _ret_tables(c):
    log_g = jnp.log(1.0 - 2.0 ** (-5.0 - jnp.arange(RET_HEADS, dtype=F32)))
    idx = jnp.arange(c, dtype=F32)
    diff = idx[:, None] - idx[None, :]
    dmask = jnp.where(diff >= 0, jnp.exp(log_g[:, None, None] * jnp.maximum(diff, 0.0)), 0.0)
    q_dec = jnp.exp(log_g[:, None] * (idx[None, :] + 1.0))
    k_dec = jnp.exp(log_g[:, None] * (c - 1.0 - idx[None, :]))
    c_dec = jnp.exp(log_g * c)
    qd = jnp.repeat(q_dec.T, RET_DK, axis=1)
    kd = jnp.repeat(k_dec.T, RET_DK, axis=1)
    cd = jnp.broadcast_to(jnp.repeat(c_dec, RET_DK).reshape(N_PAIRS, LANES, 1), (N_PAIRS, LANES, LANES))
    blk = np.arange(LANES) // HALF
    bd = jnp.asarray((blk[:, None] == blk[None, :]).astype(np.float32))
    return dmask, qd, kd, cd, bd


def _ret_call(rq, rk, rv, rg, gn, s0, *, c):
    b, t, w = rq.shape
    c = min(c, t)
    assert t % c == 0
    dmask, qd, kd, cd, bd = _ret_tables(c)
    eye = jnp.eye(LANES, dtype=BF16)
    tok = lambda i, j: (i, j, 0)
    args = [rq, rk, rv, rg, dmask, qd, kd, cd, bd, eye, gn]
    specs = [pl.BlockSpec((None, c, w), tok)] * 4 + [_const_spec(a.shape) for a in args[4:]]
    if s0 is not None:
        args.append(s0)
        specs.append(pl.BlockSpec((None, N_PAIRS, LANES, LANES), lambda i, j: (i, 0, 0, 0)))
    return pl.pallas_call(
        functools.partial(_ret_body, has_s0=s0 is not None),
        grid=(b, t // c),
        in_specs=specs,
        out_specs=[pl.BlockSpec((None, c, w), tok),
                   pl.BlockSpec((None, N_PAIRS, LANES, LANES), lambda i, j: (i, 0, 0, 0))],
        out_shape=[jax.ShapeDtypeStruct((b, t, w), BF16),
                   jax.ShapeDtypeStruct((b, N_PAIRS, LANES, LANES), F32)],
        compiler_params=_params(2),
        name="retention",
    )(*args)


def _ssd_body(*refs, has_h0):
    (z_ref, xbc_ref, sm_ref, cw_ref, cb_ref, dtb_ref, alog_ref, dsk_ref, ng_ref,
     tri_ref, eye_ref, conv0_ref) = refs[:12]
    h0_ref = refs[12] if has_h0 else None
    o_ref, h_ref, xbuf, st = refs[-4:]
    rows = z_ref.shape[0]
    ci = pl.program_id(1)

    @pl.when(ci == 0)
    def _():
        xbuf[0:CONV_PAD, :] = conv0_ref[...]
        for g in range(SSM_GROUPS):
            st[g] = h0_ref[g].T if has_h0 else jnp.zeros(st.shape[1:], F32)

    xbuf[CONV_PAD:CONV_PAD + rows, :] = xbc_ref[...]
    conv = cb_ref[...]
    for wi in range(CONV_W):
        off = CONV_PAD - (CONV_W - 1) + wi
        conv = conv + cw_ref[wi:wi + 1, :] * xbuf[off:off + rows, :]
    xbuf[0:CONV_PAD, :] = xbuf[rows:rows + CONV_PAD, :]
    act = _silu(conv)
    xs = act[:, :SSM_D_INNER]

    lo = _lane_lo(rows)
    lane = lax.broadcasted_iota(jnp.int32, (rows, LANES), 1)
    lane1 = lax.broadcasted_iota(jnp.int32, (1, LANES), 1)
    dt_raw = jnp.where(lane < DT_LANES, sm_ref[...], 0.0) + dtb_ref[...]
    dtv = jnp.maximum(dt_raw, 0.0) + jnp.log1p(jnp.exp(-jnp.abs(dt_raw)))
    a_neg = jnp.where(lane1 < DT_LANES, -jnp.exp(alog_ref[...]), 0.0)
    dta = dtv * a_neg
    eye = eye_ref[...]
    a_cum = _exact01(tri_ref[...], dta)
    a_cum_t = _exact01_nt(eye, a_cum)
    exp_a = jnp.exp(a_cum)
    a_last = a_cum[rows - 1:rows, :]
    w_dec = jnp.exp(a_last - a_cum)
    e_last = jnp.exp(a_last)
    r_i = lax.broadcasted_iota(jnp.int32, (rows, rows), 0)
    c_i = lax.broadcasted_iota(jnp.int32, (rows, rows), 1)
    causal = r_i >= c_i
    lo1 = lane1 < HALF

    def pair_lanes(v, h):
        sel = lo if v.shape[0] == rows else lo1
        return jnp.where(sel, v[:, h:h + 1], v[:, h + 1:h + 2])

    for g in range(SSM_GROUPS):
        b_off = SSM_D_INNER + g * SSM_STATE
        c_off = SSM_D_INNER + SSM_GROUPS * SSM_STATE + g * SSM_STATE
        bmb = _bf(act[:, b_off:b_off + SSM_STATE])
        cmb = _bf(act[:, c_off:c_off + SSM_STATE])
        scores = _dot_nt(cmb, bmb)
        y_off = _dot(cmb, _bf(st[g]))
        ssq = jnp.zeros((rows, 1), F32)
        gated, xw, e_row = [], [], []
        for jj in range(N_PAIRS):
            j = g * N_PAIRS + jj
            sl = slice(j * LANES, (j + 1) * LANES)
            h0 = 2 * j
            xsl = xs[:, sl]
            xdt = xsl * pair_lanes(dtv, h0)
            y = y_off[:, jj * LANES:(jj + 1) * LANES] * pair_lanes(exp_a, h0) + dsk_ref[:, sl] * xsl
            for e in range(2):
                h = h0 + e
                seg = a_cum[:, h:h + 1] - a_cum_t[h:h + 1, :]
                dec = jnp.where(causal, jnp.exp(jnp.minimum(seg, 0.0)), 0.0)
                keep = lo if e == 0 else jnp.logical_not(lo)
                y = y + _dot(_bf(scores * dec), _bf(jnp.where(keep, xdt, 0.0)))
            zs = z_ref[:, sl]
            gt = y * _silu(zs)
            ssq = ssq + jnp.sum(gt * gt, axis=-1, keepdims=True)
            gated.append(gt)
            xw.append(_bf(xdt * pair_lanes(w_dec, h0)))
            e_row.append(pair_lanes(e_last, h0))
        rs = lax.rsqrt(ssq * (1.0 / (SSM_D_INNER // SSM_GROUPS)) + EPS)
        for jj in range(N_PAIRS):
            sl = slice((g * N_PAIRS + jj) * LANES, (g * N_PAIRS + jj + 1) * LANES)
            o_ref[:, sl] = _bf(gated[jj] * rs * ng_ref[:, sl])
        bm_t = _bf(_dot_nt(eye, bmb))
        upd = _dot(bm_t, jnp.concatenate(xw, axis=1))
        st[g] = st[g] * jnp.concatenate(e_row, axis=1) + upd

    @pl.when(ci == pl.num_programs(1) - 1)
    def _():
        for g in range(SSM_GROUPS):
            h_ref[g] = st[g].T


def _ssd_call(z, xbc, small, w, conv0, h0, *, c):
    b, t, _ = z.shape
    c = min(c, t)
    assert t % c == 0 and c >= CONV_PAD
    tri = jnp.asarray(np.tril(np.ones((c, c), np.float32)), BF16)
    eye = jnp.eye(LANES, dtype=BF16)
    gw = SSM_D_INNER // SSM_GROUPS
    tok = lambda i, j: (i, j, 0)
    consts = [w["conv_w"], w["conv_b"], w["dt_bias"], w["a_log"], w["d_skip"], w["ssm_norm"], tri, eye]
    args = [z, xbc, small] + consts + [conv0]
    specs = ([pl.BlockSpec((None, c, SSM_D_INNER), tok), pl.BlockSpec((None, c, CONV_DIM), tok),
              pl.BlockSpec((None, c, LANES), tok)] + [_const_spec(a.shape) for a in consts]
             + [pl.BlockSpec((None, CONV_PAD, CONV_DIM), lambda i, j: (i, 0, 0))])
    if h0 is not None:
        args.append(h0)
        specs.append(pl.BlockSpec((None, SSM_GROUPS, gw, SSM_STATE), lambda i, j: (i, 0, 0, 0)))
    return pl.pallas_call(
        functools.partial(_ssd_body, has_h0=h0 is not None),
        grid=(b, t // c),
        in_specs=specs,
        out_specs=[pl.BlockSpec((None, c, SSM_D_INNER), tok),
                   pl.BlockSpec((None, SSM_GROUPS, gw, SSM_STATE), lambda i, j: (i, 0, 0, 0))],
        out_shape=[jax.ShapeDtypeStruct((b, t, SSM_D_INNER), BF16),
                   jax.ShapeDtypeStruct((b, SSM_GROUPS, gw, SSM_STATE), F32)],
        scratch_shapes=[pltpu.VMEM((CONV_PAD + c, CONV_DIM), F32),
                        pltpu.VMEM((SSM_GROUPS, SSM_STATE, gw), F32)],
        compiler_params=_params(2),
        name="ssd",
    )(*args)


def _pair_output(acc0, acc1, lo):
    return jnp.where(lo, acc0 / pltpu.roll(acc0, HALF, 1), pltpu.roll(acc1, HALF, 1) / acc1)


def _mla_prompt_body(q_ref, k_ref, v_ref, o_ref, m_sc, acc_sc, *, tq):
    qi = pl.program_id(1)
    m_sc[...] = jnp.full(m_sc.shape, -jnp.inf, F32)
    acc_sc[...] = jnp.zeros(acc_sc.shape, F32)
    r_i = lax.broadcasted_iota(jnp.int32, (tq, tq), 0)
    c_i = lax.broadcasted_iota(jnp.int32, (tq, tq), 1)
    visible = lax.shift_right_logical(c_i, 6) <= lax.shift_right_logical(r_i, 6)

    def step(t, diagonal):
        off = pl.multiple_of(t * tq, tq)
        for h in range(MLA_HEADS):
            hs = slice(h * LANES, (h + 1) * LANES)
            s = _dot_nt(q_ref[:, hs], k_ref[pl.ds(off, tq), hs])
            if diagonal:
                s = jnp.where(visible, s, -jnp.inf)
            m_old = m_sc[h]
            m_new = jnp.maximum(m_old, jnp.max(s, axis=-1, keepdims=True))
            alpha = jnp.exp(m_old - m_new)
            p = jnp.exp(s - jnp.concatenate([m_new] * (tq // LANES), axis=1))
            acc_sc[h] = acc_sc[h] * alpha + _dot(_bf(p), v_ref[pl.ds(off, tq), hs])
            m_sc[h] = m_new

    def body(t, carry):
        step(t, False)
        return carry

    lax.fori_loop(0, qi, body, 0)
    step(qi, True)
    lo = _lane_lo(tq)
    for j in range(N_PAIRS):
        o_ref[:, j * LANES:(j + 1) * LANES] = _bf(_pair_output(acc_sc[2 * j], acc_sc[2 * j + 1], lo))


def _mla_prompt_call(q, k, v, *, tq):
    b, t, _ = q.shape
    tq = min(tq, t)
    assert t % tq == 0 and tq % LANES == 0 and CHUNK == 64
    kw = MLA_HEADS * LANES
    return pl.pallas_call(
        functools.partial(_mla_prompt_body, tq=tq),
        grid=(b, t // tq),
        in_specs=[pl.BlockSpec((None, tq, kw), lambda i, j: (i, j, 0)),
                  pl.BlockSpec((None, t, kw), lambda i, j: (i, 0, 0)),
                  pl.BlockSpec((None, t, kw), lambda i, j: (i, 0, 0))],
        out_specs=pl.BlockSpec((None, tq, MLA_HEADS * MLA_DV), lambda i, j: (i, j, 0)),
        out_shape=jax.ShapeDtypeStruct((b, t, MLA_HEADS * MLA_DV), BF16),
        scratch_shapes=[pltpu.VMEM((MLA_HEADS, tq, LANES), F32), pltpu.VMEM((MLA_HEADS, tq, LANES), F32)],
        compiler_params=_params(2),
        name="mla_prompt",
    )(q, k, v)


def _mla_sample_body(q_ref, kp_ref, vp_ref, kn_ref, vn_ref, visp_ref, visn_ref, o_ref):
    lo = _lane_lo(q_ref.shape[0])
    vis_p = visp_ref[...] > 0.5
    vis_n = visn_ref[...] > 0.5
    for j in range(N_PAIRS):
        accs = []
        for h in (2 * j, 2 * j + 1):
            hs = slice(h * LANES, (h + 1) * LANES)
            qh = q_ref[:, hs]
            s_p = jnp.where(vis_p, _dot_nt(qh, kp_ref[:, hs]), -jnp.inf)
            s_n = jnp.where(vis_n, _dot_nt(qh, kn_ref[:, hs]), -jnp.inf)
            m = jnp.maximum(jnp.max(s_p, axis=-1, keepdims=True), jnp.max(s_n, axis=-1, keepdims=True))
            accs.append(_dot(_bf(jnp.exp(s_p - m)), vp_ref[:, hs]) + _dot(_bf(jnp.exp(s_n - m)), vn_ref[:, hs]))
        o_ref[:, j * LANES:(j + 1) * LANES] = _bf(_pair_output(accs[0], accs[1], lo))


def _mla_sample_call(q, k_past, v_past, k_new, v_new, q_pos, past_pos):
    b, t, _ = q.shape
    s = k_past.shape[1]
    vis_p = jnp.asarray((past_pos[None, :] // CHUNK <= q_pos[:, None] // CHUNK).astype(np.float32))
    vis_n = jnp.asarray((q_pos[None, :] // CHUNK <= q_pos[:, None] // CHUNK).astype(np.float32))
    kw, vw = MLA_HEADS * LANES, MLA_HEADS * MLA_DV
    bat = lambda i: (i, 0, 0)
    return pl.pallas_call(
        _mla_sample_body,
        grid=(b,),
        in_specs=[pl.BlockSpec((None, t, kw), bat), pl.BlockSpec((None, s, kw), bat),
                  pl.BlockSpec((None, s, kw), bat), pl.BlockSpec((None, t, kw), bat),
                  pl.BlockSpec((None, t, kw), bat), _const_spec(vis_p.shape), _const_spec(vis_n.shape)],
        out_specs=pl.BlockSpec((None, t, vw), bat),
        out_shape=jax.ShapeDtypeStruct((b, t, vw), BF16),
        compiler_params=_params(1),
        name="mla_sample",
    )(q, k_past, v_past, k_new, v_new, vis_p, vis_n)


def _rope_tables(pos):
    posf = pos.astype(F32)
    th_r = 1.0 / (10000.0 ** jnp.linspace(0.0, 1.0, RET_DK // 2, dtype=F32))
    ang = posf[:, None] * th_r[None, :]
    c, s = jnp.cos(ang), jnp.sin(ang)
    zr = jnp.zeros_like(s)
    ret = [jnp.concatenate([c, c, c, c], 1), jnp.concatenate([-s, zr, -s, zr], 1), jnp.concatenate([zr, s, zr, s], 1)]
    th_m = 1.0 / (ROPE_BASE ** (jnp.arange(0, MLA_ROPE, 2, dtype=F32) / MLA_ROPE))
    angm = posf[:, None] * th_m[None, :]
    cm, sm = jnp.cos(angm), jnp.sin(angm)
    n = pos.shape[0]
    one, z64 = jnp.ones((n, MLA_NOPE), F32), jnp.zeros((n, MLA_NOPE), F32)
    z16, z32 = jnp.zeros_like(sm), jnp.zeros((n, LANES - MLA_DQK), F32)
    mla = [jnp.concatenate([one, cm, cm, z32], 1), jnp.concatenate([z64, -sm, z16, z32], 1),
           jnp.concatenate([z64, z16, sm, z32], 1)]
    return jnp.concatenate(ret + mla, axis=1)


def _layer_weights(l, p):
    w_in = p["w_in"][l]
    o = RET_COLS
    w_cq, w_ckv = w_in[:, o:o + MLA_Q_LORA], w_in[:, o + MLA_Q_LORA:o + MLA_Q_LORA + MLA_KV_LORA]
    w_kr = w_in[:, o + MLA_Q_LORA + MLA_KV_LORA:o + MLA_COLS]
    o += MLA_COLS
    w_z, w_xbc, w_dt = w_in[:, o:o + SSM_D_INNER], w_in[:, o + SSM_D_INNER:o + SSM_D_INNER + CONV_DIM], w_in[:, o + SSM_D_INNER + CONV_DIM:]
    zc = lambda n: jnp.zeros((D_MODEL, n), F32)
    w_small = jnp.concatenate([w_dt, zc(KROPE_LO - DT_LANES), w_kr, zc(LANES - KROPE_LO - MLA_ROPE)], axis=1)
    w_uq = jnp.pad(p["mla_w_uq"][l].reshape(MLA_Q_LORA, MLA_HEADS, MLA_DQK), ((0, 0), (0, 0), (0, LANES - MLA_DQK)))
    w_ukv = p["mla_w_ukv"][l].reshape(MLA_KV_LORA, MLA_HEADS, MLA_NOPE + MLA_DV)
    w_kn = jnp.pad(w_ukv[:, :, :MLA_NOPE], ((0, 0), (0, 0), (0, LANES - MLA_NOPE)))
    w_vp = jnp.pad(w_ukv[:, :, MLA_NOPE:], ((0, 0), (0, 0), (0, LANES - MLA_DV)))
    w_kv = jnp.concatenate([w_kn.reshape(MLA_KV_LORA, -1), w_vp.reshape(MLA_KV_LORA, -1)], axis=1)
    pad_gain = lambda g: jnp.pad(g, (0, LANES - MLA_DQK)).reshape(1, LANES)
    pad16 = lambda v: jnp.pad(v, (0, LANES - DT_LANES)).reshape(1, LANES)
    w_out = p["w_out"][l]
    d_ret, d_mla = RET_HEADS * RET_DV, MLA_HEADS * MLA_DV
    out = {
        "mix_norm": p["mix_norm"][l].reshape(1, -1), "w_ret": _bf(w_in[:, :RET_COLS]), "w_cq": _bf(w_cq),
        "w_ckv": _bf(w_ckv), "w_z": _bf(w_z), "w_xbc": _bf(w_xbc), "w_small": _bf(w_small),
        "q_norm": p["mla_q_norm"][l].reshape(1, -1), "kv_norm": p["mla_kv_norm"][l].reshape(1, -1),
        "w_uq": _bf(w_uq.reshape(MLA_Q_LORA, -1)), "q_gain": pad_gain(p["mla_q_gain"][l]),
        "w_kv": _bf(w_kv), "k_gain": pad_gain(p["mla_k_gain"][l]),
        "ret_norm": p["ret_norm"][l].reshape(1, -1),
        "conv_w": p["ssm_conv_w"][l], "conv_b": p["ssm_conv_b"][l].reshape(1, -1),
        "dt_bias": pad16(p["ssm_dt_bias"][l]), "a_log": pad16(p["ssm_a_log"][l]),
        "d_skip": jnp.repeat(p["ssm_d"][l], SSM_HEADDIM).reshape(1, -1), "ssm_norm": p["ssm_norm"][l].reshape(1, -1),
        "wo_ret": _bf(w_out[:d_ret]), "wo_mla": _bf(w_out[d_ret:d_ret + d_mla]), "wo_ssm": _bf(w_out[d_ret + d_mla:]),
    }
    for tag in ("ffn1", "ffn2"):
        out[tag + "_norm"] = p[tag + "_norm"][l].reshape(1, -1)
        out[tag + "_wg"] = _bf(p[tag + "_wgu"][l][:, :D_FF])
        out[tag + "_wu"] = _bf(p[tag + "_wgu"][l][:, D_FF:])
        out[tag + "_wd"] = _bf(p[tag + "_wd"][l])
    return out


def _pack_ret_state(s):
    b = s.shape[0]
    s = s.reshape(b, N_PAIRS, 2, RET_DK, RET_DV)
    z = jnp.zeros_like(s[:, :, 0])
    top = jnp.concatenate([s[:, :, 0], z], axis=-1)
    bot = jnp.concatenate([z, s[:, :, 1]], axis=-1)
    return jnp.concatenate([top, bot], axis=-2)


def _unpack_ret_state(sp):
    b = sp.shape[0]
    return jnp.stack([sp[:, :, :RET_DK, :RET_DV], sp[:, :, RET_DK:, RET_DV:]], axis=2).reshape(b, RET_HEADS, RET_DK, RET_DV)


def _layer(x, tab, w, *, ret_s0, ssm_h0, conv_buf, past, cfg):
    b, t, _ = x.shape
    n = b * t
    x1 = _ffn_call(x.reshape(n, D_MODEL), [], w["ffn1_norm"], w["ffn1_wg"], w["ffn1_wu"], w["ffn1_wd"], tm=cfg["tm_ffn"])
    rq, rk, rv, rg, mq, ckv, small, z, xbc, kk, vv = _inproj_call(x1, tab, w, tm=cfg["tm_in"], t_per_batch=t)
    r3 = lambda a: a.reshape(b, t, a.shape[-1])
    o_ret, s_ret = _ret_call(r3(rq), r3(rk), r3(rv), r3(rg), w["ret_norm"],
                             None if ret_s0 is None else _pack_ret_state(ret_s0), c=cfg["c_ret"])
    if conv_buf is None:
        conv0 = jnp.zeros((b, CONV_PAD, CONV_DIM), F32)
    else:
        conv0 = jnp.concatenate([jnp.zeros((b, CONV_PAD - (CONV_W - 1), CONV_DIM), F32), conv_buf], axis=1)
    h0 = None if ssm_h0 is None else ssm_h0.reshape(b, SSM_GROUPS, -1, SSM_STATE)
    o_ssm, h_ssm = _ssd_call(r3(z), r3(xbc), r3(small), w, conv0, h0, c=cfg["c_ssd"])
    if past is None:
        o_mla = _mla_prompt_call(r3(mq), r3(kk), r3(vv), tq=cfg["tq"])
    else:
        k_past, v_past, q_pos, past_pos = past
        o_mla = _mla_sample_call(r3(mq), k_past, v_past, r3(kk), r3(vv), q_pos, past_pos)
    f2 = lambda a: a.reshape(n, a.shape[-1])
    x3 = _ffn_call(x1, [(f2(o_ret), w["wo_ret"]), (f2(o_mla), w["wo_mla"]), (f2(o_ssm), w["wo_ssm"])],
                   w["ffn2_norm"], w["ffn2_wg"], w["ffn2_wu"], w["ffn2_wd"], tm=cfg["tm_ffn"])
    xbc3 = r3(xbc)
    if conv_buf is None:
        assert t >= CONV_W - 1
        conv_new = xbc3[:, t - (CONV_W - 1):]
    else:
        conv_new = jnp.concatenate([conv_buf, xbc3], axis=1)[:, t:]
    state = (r3(ckv), r3(small)[:, :, KROPE_LO:KROPE_LO + MLA_ROPE], _unpack_ret_state(s_ret),
             h_ssm.reshape(b, SSM_HEADS, SSM_HEADDIM, SSM_STATE), conv_new)
    return x3.reshape(b, t, D_MODEL), state


PROMPT_CFG = dict(tm_ffn=512, tm_in=256, c_ret=128, c_ssd=128, tq=256)
SAMPLE_CFG = dict(tm_ffn=256, tm_in=256, c_ret=128, c_ssd=128, tq=256)


def kernel(x_prompt, x_sample, cache_mla_ckv, cache_mla_krope, state_ret, state_ssm, state_conv, ffn1_norm, ffn1_wgu, ffn1_wd, mix_norm, w_in, ret_norm, mla_q_norm, mla_kv_norm, mla_w_uq, mla_w_ukv, mla_q_gain, mla_k_gain, ssm_conv_w, ssm_conv_b, ssm_dt_bias, ssm_a_log, ssm_d, ssm_norm, w_out, ffn2_norm, ffn2_wgu, ffn2_wd):
    p = dict(ffn1_norm=ffn1_norm, ffn1_wgu=ffn1_wgu, ffn1_wd=ffn1_wd, mix_norm=mix_norm, w_in=w_in, ret_norm=ret_norm,
             mla_q_norm=mla_q_norm, mla_kv_norm=mla_kv_norm, mla_w_uq=mla_w_uq, mla_w_ukv=mla_w_ukv,
             mla_q_gain=mla_q_gain, mla_k_gain=mla_k_gain, ssm_conv_w=ssm_conv_w, ssm_conv_b=ssm_conv_b,
             ssm_dt_bias=ssm_dt_bias, ssm_a_log=ssm_a_log, ssm_d=ssm_d, ssm_norm=ssm_norm, w_out=w_out,
             ffn2_norm=ffn2_norm, ffn2_wgu=ffn2_wgu, ffn2_wd=ffn2_wd)
    depth = ffn1_norm.shape[0]
    b_s, t_s, _ = x_sample.shape
    t_p = x_prompt.shape[1]
    past = cache_mla_ckv.shape[2]
    q_pos_s = past + np.arange(t_s)
    past_pos = np.arange(past)
    tab_p = _rope_tables(jnp.arange(t_p, dtype=jnp.int32))
    tab_s = jnp.tile(_rope_tables(jnp.asarray(q_pos_s, jnp.int32)), (b_s, 1))
    tab_past = _rope_tables(jnp.arange(past, dtype=jnp.int32))[:, 3 * LANES:]
    yp, ys = x_prompt, x_sample
    new_p, new_s = [[] for _ in range(5)], [[] for _ in range(5)]
    for l in range(depth):
        w = _layer_weights(l, p)
        yp, st_p = _layer(yp, tab_p, w, ret_s0=None, ssm_h0=None, conv_buf=None, past=None, cfg=PROMPT_CFG)
        k_past, v_past = _kvprep_call(cache_mla_ckv[l], cache_mla_krope[l], w["w_kv"], w["k_gain"], tab_past, tm=512)
        ys, st_s = _layer(ys, tab_s, w, ret_s0=state_ret[l], ssm_h0=state_ssm[l], conv_buf=state_conv[l],
                          past=(k_past, v_past, q_pos_s, past_pos), cfg=SAMPLE_CFG)
        for i in range(5):
            new_p[i].append(st_p[i])
            new_s[i].append(st_s[i])
    return (yp, ys) + tuple(jnp.stack(a) for a in new_p) + tuple(jnp.stack(a) for a in new_s)
```

```python
import functools

import numpy as np
import jax
import jax.numpy as jnp
from jax import lax
from jax.experimental import pallas as pl
from jax.experimental.pallas import tpu as pltpu

D_MODEL = 1024
D_FF = 2816
CHUNK = 64
EPS = 1e-6
RET_HEADS = 8
RET_DK = 64
RET_DV = 64
MLA_HEADS = 8
MLA_Q_LORA = 384
MLA_KV_LORA = 256
MLA_NOPE = 64
MLA_ROPE = 32
MLA_DQK = MLA_NOPE + MLA_ROPE
MLA_DV = 64
ROPE_BASE = 10000.0
SSM_HEADS = 16
SSM_HEADDIM = 64
SSM_D_INNER = SSM_HEADS * SSM_HEADDIM
SSM_GROUPS = 2
SSM_STATE = 128
CONV_W = 4
CONV_DIM = SSM_D_INNER + 2 * SSM_GROUPS * SSM_STATE
RET_COLS = 2 * RET_HEADS * RET_DK + 2 * RET_HEADS * RET_DV
MLA_COLS = MLA_Q_LORA + MLA_KV_LORA + MLA_ROPE

LANES = 128
HALF = LANES // 2
N_PAIRS = 4
RET_GROUPS = 2
RET_GH = RET_HEADS // RET_GROUPS
RET_GW = RET_GH * RET_DK
LOG2E = 1.4426950408889634
CONV_PAD = 8
DT_LANES = SSM_HEADS
KROPE_LO = MLA_NOPE
V7X_VMEM_LIMIT_BYTES = 56 * 1024 * 1024

F32 = jnp.float32
BF16 = jnp.bfloat16


def _bf(x):
    return x.astype(BF16)


def _dot(a, b):
    return jnp.dot(a, b, preferred_element_type=F32)


def _dot_nt(a, b):
    return lax.dot_general(a, b, (((1,), (1,)), ((), ())), preferred_element_type=F32)


def _sigmoid(x):
    return 1.0 / (1.0 + jnp.exp(-x))


def _silu(x):
    return x * _sigmoid(x)


def _rms(x):
    return x * lax.rsqrt(jnp.mean(x * x, axis=-1, keepdims=True) + EPS)


def _split3(x):
    hi = _bf(x)
    r = x - hi.astype(F32)
    mid = _bf(r)
    lo = _bf(r - mid.astype(F32))
    return hi, mid, lo


def _exact01(m01, x):
    hi, mid, lo = _split3(x)
    return _dot(m01, hi) + _dot(m01, mid) + _dot(m01, lo)


def _exact01_nt(m01, x):
    hi, mid, lo = _split3(x)
    return _dot_nt(m01, hi) + _dot_nt(m01, mid) + _dot_nt(m01, lo)


def _exact01_tn(x, m01):
    hi, mid, lo = _split3(x)
    return _dot(hi, m01) + _dot(mid, m01) + _dot(lo, m01)


def _dot_hi_mid(x, m01):
    hi = _bf(x)
    return _dot(hi, m01) + _dot(_bf(x - hi.astype(F32)), m01)


def _lane_lo(rows):
    return lax.broadcasted_iota(jnp.int32, (rows, LANES), 1) < HALF


def _conv_silu(xbuf, x_new, cw_ref, cb_ref):
    rows = x_new.shape[0]
    xbuf[CONV_PAD:CONV_PAD + rows, :] = x_new
    conv = cb_ref[...]
    for wi in range(CONV_W):
        off = CONV_PAD - (CONV_W - 1) + wi
        conv = conv + cw_ref[wi:wi + 1, :] * xbuf[off:off + rows, :]
    xbuf[0:CONV_PAD, :] = xbuf[rows:rows + CONV_PAD, :]
    return _silu(conv)


def _const_spec(shape):
    nd = len(shape)
    return pl.BlockSpec(shape, lambda *_: (0,) * nd, pipeline_mode=pl.Buffered(1))


def _params(n_axes):
    return pltpu.CompilerParams(dimension_semantics=("arbitrary",) * n_axes,
                                vmem_limit_bytes=V7X_VMEM_LIMIT_BYTES)


def _ffn_body(*refs, n_pre, fc):
    x_ref = refs[0]
    pre = [(refs[1 + 2 * i], refs[2 + 2 * i]) for i in range(n_pre)]
    g_ref, wg_ref, wu_ref, wd_ref, o_ref = refs[1 + 2 * n_pre:]
    x = x_ref[...]
    for m_ref, w_ref in pre:
        x = x + _dot(m_ref[...], w_ref[...])
    xn = _bf(_rms(x) * g_ref[...])
    acc = None
    for c in range(D_FF // fc):
        gate = _dot(xn, wg_ref[:, c * fc:(c + 1) * fc])
        up = _dot(xn, wu_ref[:, c * fc:(c + 1) * fc])
        part = _dot(_bf(_silu(gate) * up), wd_ref[c * fc:(c + 1) * fc, :])
        acc = part if acc is None else acc + part
    o_ref[...] = x + 0.5 * acc


def _ffn_call(x, pre, gain, wg, wu, wd, *, tm, fc=256):
    n = x.shape[0]
    tm = min(tm, n)
    assert n % tm == 0 and D_FF % fc == 0
    args, specs = [x], [pl.BlockSpec((tm, D_MODEL), lambda i: (i, 0))]
    for m, w in pre:
        args += [m, w]
        specs += [pl.BlockSpec((tm, m.shape[1]), lambda i: (i, 0)), _const_spec(w.shape)]
    args += [gain, wg, wu, wd]
    specs += [_const_spec(gain.shape), _const_spec(wg.shape), _const_spec(wu.shape), _const_spec(wd.shape)]
    return pl.pallas_call(
        functools.partial(_ffn_body, n_pre=len(pre), fc=fc),
        grid=(n // tm,),
        in_specs=specs,
        out_specs=pl.BlockSpec((tm, D_MODEL), lambda i: (i, 0)),
        out_shape=jax.ShapeDtypeStruct((n, D_MODEL), F32),
        compiler_params=_params(1),
        name="ffn",
    )(*args)


def _kv_from_latent(ckv_n, krope_slab, krope_sw_slab, wkv_ref, kgain, kgain_sw, cos, sin_s, k_ref, v_ref):
    kvp = _dot(_bf(ckv_n), wkv_ref[...])
    ones_hi = jnp.where(_lane_lo(kvp.shape[0]), 0.0, 1.0)
    g_cos = kgain * cos
    rot = krope_sw_slab * (kgain_sw * sin_s)
    for h in range(MLA_HEADS):
        kh = kvp[:, h * LANES:(h + 1) * LANES] + krope_slab
        ss = jnp.sum(kh * kh, axis=-1, keepdims=True) * (1.0 / MLA_DQK)
        k_ref[:, h * LANES:(h + 1) * LANES] = _bf((kh * g_cos + rot) * lax.rsqrt(ss + EPS))
        vs = slice((MLA_HEADS + h) * LANES, (MLA_HEADS + h + 1) * LANES)
        v_ref[:, h * LANES:(h + 1) * LANES] = _bf(kvp[:, vs] + ones_hi)


def _kvprep_body(ckv_ref, kr_ref, place_ref, wkv_ref, kgain_ref, tab_ref, k_ref, v_ref):
    slabs = _exact01_tn(kr_ref[...], place_ref[...])
    tab = tab_ref[...]
    kgain = kgain_ref[...]
    _kv_from_latent(ckv_ref[...], slabs[:, :LANES], slabs[:, LANES:], wkv_ref, kgain[0:1], kgain[1:2],
                    tab[:, :LANES], tab[:, LANES:], k_ref, v_ref)


def _swap_lanes():
    idx = np.arange(LANES)
    half = MLA_ROPE // 2
    idx[KROPE_LO:KROPE_LO + half] += half
    idx[KROPE_LO + half:KROPE_LO + MLA_ROPE] -= half
    return idx


def _kvprep_call(ckv, krope, wkv, kgain, tab, *, tm):
    b, s, _ = ckv.shape
    tm = min(tm, s)
    assert s % tm == 0
    place = np.zeros((MLA_ROPE, 2 * LANES), np.float32)
    place[np.arange(MLA_ROPE), KROPE_LO + np.arange(MLA_ROPE)] = 1.0
    place[np.arange(MLA_ROPE), LANES + _swap_lanes()[KROPE_LO:KROPE_LO + MLA_ROPE]] = 1.0
    place = jnp.asarray(place, BF16)
    return pl.pallas_call(
        _kvprep_body,
        grid=(b, s // tm),
        in_specs=[pl.BlockSpec((None, tm, MLA_KV_LORA), lambda i, j: (i, j, 0)),
                  pl.BlockSpec((None, tm, MLA_ROPE), lambda i, j: (i, j, 0)),
                  _const_spec(place.shape), _const_spec(wkv.shape), _const_spec(kgain.shape),
                  pl.BlockSpec((tm, 2 * LANES), lambda i, j: (j, 0))],
        out_specs=[pl.BlockSpec((None, tm, MLA_HEADS * LANES), lambda i, j: (i, j, 0))] * 2,
        out_shape=[jax.ShapeDtypeStruct((b, s, MLA_HEADS * LANES), BF16)] * 2,
        compiler_params=_params(2),
        name="kvprep",
    )(ckv, krope, place, wkv, kgain, tab)


def _inproj_body(*refs, conv_per):
    (x_ref, tab_ref, gmix_ref, wret_ref, wcq_ref, wckv_ref, wz_ref, wxbc_ref, wsm_ref,
     gq_ref, gkv_ref, wuq_ref, qgain_ref, wkv_ref, kgain_ref) = refs[:15]
    (rq_ref, rk_ref, rv_ref, rg_ref, mq_ref, ckv_ref, sm_ref, z_ref, xbc_ref, k_ref, v_ref) = refs[-11 - 2 * (conv_per is not None):][:11]
    hn = _bf(_rms(x_ref[...]) * gmix_ref[...])
    tab = tab_ref[...]
    cos_r, sin_r = tab[:, 0:LANES], tab[:, LANES:2 * LANES]
    cos_m, sin_s = tab[:, 2 * LANES:3 * LANES], tab[:, 3 * LANES:4 * LANES]

    xbc = _dot(hn, wxbc_ref[...])
    if conv_per is None:
        xbc_ref[...] = xbc
    else:
        cw_ref, cb_ref, conv0_ref = refs[15:18]
        tail_ref, xbuf = refs[-2:]

        @pl.when(pl.program_id(0) % conv_per == 0)
        def _():
            xbuf[0:CONV_PAD, :] = conv0_ref[...]

        xbc_ref[...] = _conv_silu(xbuf, xbc, cw_ref, cb_ref)
        tail_ref[...] = xbuf[0:CONV_PAD, :]
    z_ref[...] = _dot(hn, wz_ref[...])

    hk = RET_HEADS * RET_DK
    ret = _dot(hn, wret_ref[...])
    for out_ref, base, scale in ((rq_ref, 0, 1.0), (rk_ref, hk, RET_DK ** -0.5)):
        for g in range(RET_GROUPS):
            lo_sl = slice(g * RET_GW, g * RET_GW + LANES)
            hi_sl = slice(g * RET_GW + LANES, (g + 1) * RET_GW)
            x1 = ret[:, base + g * RET_GW:base + g * RET_GW + LANES]
            x2 = ret[:, base + g * RET_GW + LANES:base + (g + 1) * RET_GW]
            out_ref[:, lo_sl] = _bf((x1 * cos_r - x2 * sin_r) * scale)
            out_ref[:, hi_sl] = _bf((x1 * sin_r + x2 * cos_r) * scale)
    rv_ref[...] = _bf(ret[:, 2 * hk:3 * hk])
    rg_ref[...] = _silu(ret[:, 3 * hk:])

    cq = _bf(_rms(_dot(hn, wcq_ref[...])) * gq_ref[...])
    qp = _dot(cq, wuq_ref[...])
    qgain = qgain_ref[...]
    q_scale = (MLA_DQK ** -0.5) * LOG2E
    g_cos = qgain[0:1] * cos_m * q_scale
    g_sin = qgain[1:2] * sin_s * q_scale
    for h in range(MLA_HEADS):
        qh = qp[:, h * LANES:(h + 1) * LANES]
        qs = qp[:, (MLA_HEADS + h) * LANES:(MLA_HEADS + h + 1) * LANES]
        ss = jnp.sum(qh * qh, axis=-1, keepdims=True) * (1.0 / MLA_DQK)
        mq_ref[:, h * LANES:(h + 1) * LANES] = _bf((qh * g_cos + qs * g_sin) * lax.rsqrt(ss + EPS))

    small2 = _dot(hn, wsm_ref[...])
    small = small2[:, :LANES]
    sm_ref[...] = small
    ckv_n = _rms(_dot(hn, wckv_ref[...])) * gkv_ref[...]
    ckv_ref[...] = ckv_n
    lane = lax.broadcasted_iota(jnp.int32, small.shape, 1)
    rot_lanes = (lane >= KROPE_LO) & (lane < KROPE_LO + MLA_ROPE)
    kgain = kgain_ref[...]
    _kv_from_latent(ckv_n, jnp.where(rot_lanes, small, 0.0), jnp.where(rot_lanes, small2[:, LANES:], 0.0),
                    wkv_ref, kgain[0:1], kgain[1:2], cos_m, sin_s, k_ref, v_ref)


def _inproj_call(x, tab, w, conv0, *, tm, t_per_batch):
    n = x.shape[0]
    tm = min(tm, n)
    assert n % tm == 0
    if tm <= t_per_batch:
        assert t_per_batch % tm == 0 and tab.shape[0] == t_per_batch
        per = t_per_batch // tm
        tab_spec = pl.BlockSpec((tm, 4 * LANES), lambda i: (i % per, 0))
    else:
        assert tab.shape[0] == tm and conv0 is None
        tab_spec = pl.BlockSpec((tm, 4 * LANES), lambda i: (0, 0))
    consts = [w["mix_norm"], w["w_ret"], w["w_cq"], w["w_ckv"], w["w_z"], w["w_xbc"], w["w_small"],
              w["q_norm"], w["kv_norm"], w["w_uq"], w["q_gain"], w["w_kv"], w["k_gain"]]
    widths = [(512, BF16), (512, BF16), (512, BF16), (512, F32), (MLA_HEADS * LANES, BF16),
              (MLA_KV_LORA, F32), (LANES, F32), (SSM_D_INNER, F32), (CONV_DIM, F32),
              (MLA_HEADS * LANES, BF16), (MLA_HEADS * LANES, BF16)]
    in_specs = [pl.BlockSpec((tm, D_MODEL), lambda i: (i, 0)), tab_spec] + [_const_spec(c.shape) for c in consts]
    out_specs = [pl.BlockSpec((tm, wd), lambda i: (i, 0)) for wd, _ in widths]
    out_shape = [jax.ShapeDtypeStruct((n, wd), dt) for wd, dt in widths]
    args, scratch = [x, tab] + consts, []
    if conv0 is not None:
        tail_spec = pl.BlockSpec((None, CONV_PAD, CONV_DIM), lambda i: (i // per, 0, 0))
        args += [w["conv_w"], w["conv_b"], conv0]
        in_specs += [_const_spec(w["conv_w"].shape), _const_spec(w["conv_b"].shape), tail_spec]
        out_specs.append(tail_spec)
        out_shape.append(jax.ShapeDtypeStruct(conv0.shape, F32))
        scratch = [pltpu.VMEM((CONV_PAD + tm, CONV_DIM), F32)]
    return pl.pallas_call(
        functools.partial(_inproj_body, conv_per=None if conv0 is None else per),
        grid=(n // tm,),
        in_specs=in_specs,
        out_specs=out_specs,
        out_shape=out_shape,
        scratch_shapes=scratch,
        compiler_params=_params(1),
        name="inproj",
    )(*args)


def _ret_body(*refs, has_s0, c, nsub):
    (q_ref, k_ref, v_ref, g_ref, dm_ref, qd_ref, kd_ref, cd_ref, bd_ref, eye_ref, gn_ref, hm_ref, vm_ref) = refs[:13]
    s0_ref = refs[13] if has_s0 else None
    o_ref, s_ref = refs[-2:]

    @pl.when(pl.program_id(1) == 0)
    def _():
        s_ref[...] = s0_ref[...] if has_s0 else jnp.zeros(s_ref.shape, F32)

    lo = _lane_lo(c)
    eye = eye_ref[...]
    for sub in range(nsub):
        rs = slice(sub * c, (sub + 1) * c)
        for g in range(RET_GROUPS):
            gs = slice(g * RET_GW, (g + 1) * RET_GW)
            qg, kg, vg = q_ref[rs, gs], k_ref[rs, gs], v_ref[rs, gs]
            state = s_ref[g]
            o = _dot(qg, _bf(state)) * qd_ref[:, gs]
            for hh in range(RET_GH):
                att = _dot_nt(qg * hm_ref[hh], kg) * dm_ref[g * RET_GH + hh]
                o = o + _dot(_bf(att), vg * vm_ref[hh])
            for half in range(2):
                sl = slice(g * RET_GW + half * LANES, g * RET_GW + (half + 1) * LANES)
                oh = o[:, half * LANES:(half + 1) * LANES]
                oo = oh * oh
                s_lo = jnp.sum(jnp.where(lo, oo, 0.0), axis=-1, keepdims=True)
                s_hi = jnp.sum(jnp.where(lo, 0.0, oo), axis=-1, keepdims=True)
                ss = jnp.where(lo, s_lo, s_hi) * (1.0 / RET_DV)
                o_ref[rs, sl] = _bf(oh * lax.rsqrt(ss + EPS) * gn_ref[:, sl] * g_ref[rs, sl])
            kdec = _bf(kg.astype(F32) * kd_ref[:, gs])
            kdec_t = _bf(_dot_nt(eye, kdec))
            s_ref[g] = state * cd_ref[g] + _dot(kdec_t, vg) * bd_ref[...]


def _ret_qk_head(width):
    return (np.arange(width) % LANES) // (RET_DK // 2) + RET_GH * (np.arange(width) // RET_GW)


def _ret_tables(c):
    log_g = jnp.log(1.0 - 2.0 ** (-5.0 - jnp.arange(RET_HEADS, dtype=F32)))
    idx = jnp.arange(c, dtype=F32)
    diff = idx[:, None] - idx[None, :]
    dmask = jnp.where(diff >= 0, jnp.exp(log_g[:, None, None] * jnp.maximum(diff, 0.0)), 0.0)
    q_dec = jnp.exp(log_g[:, None] * (idx[None, :] + 1.0))
    k_dec = jnp.exp(log_g[:, None] * (c - 1.0 - idx[None, :]))
    c_dec = jnp.exp(log_g * c)
    w = RET_HEADS * RET_DK
    qk_head = _ret_qk_head(w)
    v_head = np.arange(w) // RET_DV
    qd = q_dec.T[:, v_head]
    kd = k_dec.T[:, qk_head]
    cd = jnp.broadcast_to(c_dec[qk_head].reshape(RET_GROUPS, RET_GW, 1), (RET_GROUPS, RET_GW, RET_GW))
    bd = jnp.asarray((qk_head[:RET_GW, None] == v_head[None, :RET_GW]).astype(np.float32))
    hm = np.stack([np.broadcast_to(qk_head[:RET_GW] == hh, (c, RET_GW)) for hh in range(RET_GH)])
    vm = np.stack([np.broadcast_to(v_head[:RET_GW] == hh, (c, RET_GW)) for hh in range(RET_GH)])
    return dmask, qd, kd, cd, bd, jnp.asarray(hm, BF16), jnp.asarray(vm, BF16)


def _ret_call(rq, rk, rv, rg, gn, s0, *, c, nsub):
    b, t, w = rq.shape
    c = min(c, t)
    nsub = min(nsub, t // c)
    blk = c * nsub
    assert t % blk == 0
    dmask, qd, kd, cd, bd, hm, vm = _ret_tables(c)
    eye = jnp.eye(RET_GW, dtype=BF16)
    tok = lambda i, j: (i, j, 0)
    args = [rq, rk, rv, rg, dmask, qd, kd, cd, bd, eye, gn, hm, vm]
    specs = [pl.BlockSpec((None, blk, w), tok)] * 4 + [_const_spec(a.shape) for a in args[4:]]
    st_spec = pl.BlockSpec((None, RET_GROUPS, RET_GW, RET_GW), lambda i, j: (i, 0, 0, 0))
    if s0 is not None:
        args.append(s0)
        specs.append(st_spec)
    return pl.pallas_call(
        functools.partial(_ret_body, has_s0=s0 is not None, c=c, nsub=nsub),
        grid=(b, t // blk),
        in_specs=specs,
        out_specs=[pl.BlockSpec((None, blk, w), tok), st_spec],
        out_shape=[jax.ShapeDtypeStruct((b, t, w), BF16),
                   jax.ShapeDtypeStruct((b, RET_GROUPS, RET_GW, RET_GW), F32)],
        compiler_params=_params(2),
        name="retention",
    )(*args)


def _ssd_body(*refs, has_h0, pre_act, c, nsub):
    (z_ref, xbc_ref, sm_ref, cw_ref, cb_ref, dtb_ref, alog_ref, dsk_ref, ng_ref,
     tri_ref, eye_ref, spread_ref, conv0_ref) = refs[:13]
    h0_ref = refs[13] if has_h0 else None
    o_ref, h_ref, xbuf, st = refs[-4:]
    rows = z_ref.shape[0]
    ci = pl.program_id(1)

    @pl.when(ci == 0)
    def _():
        if not pre_act:
            xbuf[0:CONV_PAD, :] = conv0_ref[...]
        for g in range(SSM_GROUPS):
            st[g] = h0_ref[g].T if has_h0 else jnp.zeros(st.shape[1:], F32)

    act_all = xbc_ref[...] if pre_act else _conv_silu(xbuf, xbc_ref[...], cw_ref, cb_ref)

    lane = lax.broadcasted_iota(jnp.int32, (rows, LANES), 1)
    lane1 = lax.broadcasted_iota(jnp.int32, (1, LANES), 1)
    dt_raw = jnp.where(lane < DT_LANES, sm_ref[...], 0.0) + dtb_ref[...]
    dt_all = jnp.maximum(dt_raw, 0.0) + jnp.log1p(jnp.exp(-jnp.abs(dt_raw)))
    a_neg = jnp.where(lane1 < DT_LANES, -jnp.exp(alog_ref[...]), 0.0)
    dta_all = dt_all * a_neg
    ldt_all = jnp.maximum(jnp.log(dt_all), -1e30)
    eye = eye_ref[...]
    lo = _lane_lo(c)
    lo1 = lane1 < HALF
    m_lo = _bf(jnp.where(lo, 1.0, 0.0))
    m_hi = _bf(jnp.where(lo, 0.0, 1.0))
    r_i = lax.broadcasted_iota(jnp.int32, (c, c), 0)
    c_i = lax.broadcasted_iota(jnp.int32, (c, c), 1)
    causal = r_i >= c_i

    def pair_lanes(v, h):
        return jnp.where(lo1, v[:, h:h + 1], v[:, h + 1:h + 2])

    for sub in range(nsub):
        r0 = sub * c
        rs_ = slice(r0, r0 + c)
        act = act_all[r0:r0 + c]
        xs = act[:, :SSM_D_INNER]
        dtv = dt_all[r0:r0 + c]
        a_cum = _exact01(tri_ref[...], dta_all[r0:r0 + c])
        row_t = _exact01_nt(eye, a_cum - ldt_all[r0:r0 + c])
        a_last = a_cum[c - 1:c, :]
        e_last = jnp.exp(a_last)
        exp_a = _dot_hi_mid(jnp.exp(a_cum), spread_ref[...])
        w_dt = _dot_hi_mid(jnp.exp(a_last - a_cum) * dtv, spread_ref[...])
        for g in range(SSM_GROUPS):
            b_off = SSM_D_INNER + g * SSM_STATE
            c_off = SSM_D_INNER + SSM_GROUPS * SSM_STATE + g * SSM_STATE
            bmb = _bf(act[:, b_off:b_off + SSM_STATE])
            cmb = _bf(act[:, c_off:c_off + SSM_STATE])
            scores = _dot_nt(cmb, bmb)
            y_off = _dot(cmb, _bf(st[g]))
            ssq = jnp.zeros((c, 1), F32)
            gated, xw, e_row = [], [], []
            for jj in range(N_PAIRS):
                j = g * N_PAIRS + jj
                sl = slice(j * LANES, (j + 1) * LANES)
                h0 = 2 * j
                xsl = xs[:, sl]
                xb = _bf(xsl)
                y = y_off[:, jj * LANES:(jj + 1) * LANES] * exp_a[:, sl] + dsk_ref[:, sl] * xsl
                for e, msk in enumerate((m_lo, m_hi)):
                    h = h0 + e
                    seg = jnp.where(causal, a_cum[:, h:h + 1] - row_t[h:h + 1, :], -jnp.inf)
                    y = y + _dot(_bf(scores * jnp.exp(seg)), xb * msk)
                zs = z_ref[rs_, sl]
                gt = y * _silu(zs)
                ssq = ssq + jnp.sum(gt * gt, axis=-1, keepdims=True)
                gated.append(gt)
                xw.append(_bf(xsl * w_dt[:, sl]))
                e_row.append(pair_lanes(e_last, h0))
            rs = lax.rsqrt(ssq * (1.0 / (SSM_D_INNER // SSM_GROUPS)) + EPS)
            for jj in range(N_PAIRS):
                sl = slice((g * N_PAIRS + jj) * LANES, (g * N_PAIRS + jj + 1) * LANES)
                o_ref[rs_, sl] = _bf(gated[jj] * rs * ng_ref[:, sl])
            bm_t = _bf(_dot_nt(eye, bmb))
            upd = _dot(bm_t, jnp.concatenate(xw, axis=1))
            st[g] = st[g] * jnp.concatenate(e_row, axis=1) + upd

    @pl.when(ci == pl.num_programs(1) - 1)
    def _():
        for g in range(SSM_GROUPS):
            h_ref[g] = st[g].T


def _ssd_call(z, xbc, small, w, conv0, h0, *, c, nsub, pre_act):
    b, t, _ = z.shape
    c = min(c, t)
    nsub = min(nsub, t // c)
    blk = c * nsub
    assert t % blk == 0 and c >= CONV_PAD
    tri = jnp.asarray(np.tril(np.ones((c, c), np.float32)), BF16)
    eye = jnp.eye(LANES, dtype=BF16)
    spread = np.zeros((LANES, SSM_D_INNER), np.float32)
    spread[np.arange(SSM_D_INNER) // SSM_HEADDIM, np.arange(SSM_D_INNER)] = 1.0
    spread = jnp.asarray(spread, BF16)
    gw = SSM_D_INNER // SSM_GROUPS
    tok = lambda i, j: (i, j, 0)
    consts = [w["conv_w"], w["conv_b"], w["dt_bias"], w["a_log"], w["d_skip"], w["ssm_norm"], tri, eye, spread]
    args = [z, xbc, small] + consts + [conv0]
    specs = ([pl.BlockSpec((None, blk, SSM_D_INNER), tok), pl.BlockSpec((None, blk, CONV_DIM), tok),
              pl.BlockSpec((None, blk, LANES), tok)] + [_const_spec(a.shape) for a in consts]
             + [pl.BlockSpec((None, CONV_PAD, CONV_DIM), lambda i, j: (i, 0, 0))])
    if h0 is not None:
        args.append(h0)
        specs.append(pl.BlockSpec((None, SSM_GROUPS, gw, SSM_STATE), lambda i, j: (i, 0, 0, 0)))
    return pl.pallas_call(
        functools.partial(_ssd_body, has_h0=h0 is not None, pre_act=pre_act, c=c, nsub=nsub),
        grid=(b, t // blk),
        in_specs=specs,
        out_specs=[pl.BlockSpec((None, blk, SSM_D_INNER), tok),
                   pl.BlockSpec((None, SSM_GROUPS, gw, SSM_STATE), lambda i, j: (i, 0, 0, 0))],
        out_shape=[jax.ShapeDtypeStruct((b, t, SSM_D_INNER), BF16),
                   jax.ShapeDtypeStruct((b, SSM_GROUPS, gw, SSM_STATE), F32)],
        scratch_shapes=[pltpu.VMEM((CONV_PAD + (0 if pre_act else blk), CONV_DIM), F32),
                        pltpu.VMEM((SSM_GROUPS, SSM_STATE, gw), F32)],
        compiler_params=_params(2),
        name="ssd",
    )(*args)


def _pair_output(acc0, acc1, lo):
    return jnp.where(lo, acc0 / pltpu.roll(acc0, HALF, 1), pltpu.roll(acc1, HALF, 1) / acc1)


def _mla_prompt_body(q_ref, k_ref, v_ref, o_ref, m_sc, acc_sc, *, tq):
    qi = pl.program_id(1)
    m_sc[...] = jnp.full(m_sc.shape, -jnp.inf, F32)
    acc_sc[...] = jnp.zeros(acc_sc.shape, F32)
    r_i = lax.broadcasted_iota(jnp.int32, (tq, tq), 0)
    c_i = lax.broadcasted_iota(jnp.int32, (tq, tq), 1)
    visible = lax.shift_right_logical(c_i, 6) <= lax.shift_right_logical(r_i, 6)

    def step(t, diagonal):
        off = pl.multiple_of(t * tq, tq)
        for h in range(MLA_HEADS):
            hs = slice(h * LANES, (h + 1) * LANES)
            s = _dot_nt(q_ref[:, hs], k_ref[pl.ds(off, tq), hs])
            if diagonal:
                s = jnp.where(visible, s, -jnp.inf)
            m_old = m_sc[h]
            m_new = jnp.maximum(m_old, jnp.max(s, axis=-1, keepdims=True))
            alpha = jnp.exp2(m_old - m_new)
            p = jnp.exp2(s - jnp.concatenate([m_new] * (tq // LANES), axis=1))
            acc_sc[h] = acc_sc[h] * alpha + _dot(_bf(p), v_ref[pl.ds(off, tq), hs])
            m_sc[h] = m_new

    def body(t, carry):
        step(t, False)
        return carry

    lax.fori_loop(0, qi, body, 0)
    step(qi, True)
    lo = _lane_lo(tq)
    for j in range(N_PAIRS):
        o_ref[:, j * LANES:(j + 1) * LANES] = _bf(_pair_output(acc_sc[2 * j], acc_sc[2 * j + 1], lo))


def _mla_prompt_call(q, k, v, *, tq):
    b, t, _ = q.shape
    tq = min(tq, t)
    assert t % tq == 0 and tq % LANES == 0 and CHUNK == 64
    kw = MLA_HEADS * LANES
    return pl.pallas_call(
        functools.partial(_mla_prompt_body, tq=tq),
        grid=(b, t // tq),
        in_specs=[pl.BlockSpec((None, tq, kw), lambda i, j: (i, j, 0)),
                  pl.BlockSpec((None, t, kw), lambda i, j: (i, 0, 0)),
                  pl.BlockSpec((None, t, kw), lambda i, j: (i, 0, 0))],
        out_specs=pl.BlockSpec((None, tq, MLA_HEADS * MLA_DV), lambda i, j: (i, j, 0)),
        out_shape=jax.ShapeDtypeStruct((b, t, MLA_HEADS * MLA_DV), BF16),
        scratch_shapes=[pltpu.VMEM((MLA_HEADS, tq, LANES), F32), pltpu.VMEM((MLA_HEADS, tq, LANES), F32)],
        compiler_params=_params(2),
        name="mla_prompt",
    )(q, k, v)


def _mla_sample_body(q_ref, kp_ref, vp_ref, kn_ref, vn_ref, visp_ref, visn_ref, o_ref):
    lo = _lane_lo(q_ref.shape[0])
    vis_p = visp_ref[...] > 0.5
    vis_n = visn_ref[...] > 0.5
    for j in range(N_PAIRS):
        accs = []
        for h in (2 * j, 2 * j + 1):
            hs = slice(h * LANES, (h + 1) * LANES)
            qh = q_ref[:, hs]
            s_p = jnp.where(vis_p, _dot_nt(qh, kp_ref[:, hs]), -jnp.inf)
            s_n = jnp.where(vis_n, _dot_nt(qh, kn_ref[:, hs]), -jnp.inf)
            m = jnp.maximum(jnp.max(s_p, axis=-1, keepdims=True), jnp.max(s_n, axis=-1, keepdims=True))
            accs.append(_dot(_bf(jnp.exp2(s_p - m)), vp_ref[:, hs]) + _dot(_bf(jnp.exp2(s_n - m)), vn_ref[:, hs]))
        o_ref[:, j * LANES:(j + 1) * LANES] = _bf(_pair_output(accs[0], accs[1], lo))


def _mla_sample_call(q, k_past, v_past, k_new, v_new, q_pos, past_pos):
    b, t, _ = q.shape
    s = k_past.shape[1]
    vis_p = jnp.asarray((past_pos[None, :] // CHUNK <= q_pos[:, None] // CHUNK).astype(np.float32))
    vis_n = jnp.asarray((q_pos[None, :] // CHUNK <= q_pos[:, None] // CHUNK).astype(np.float32))
    kw, vw = MLA_HEADS * LANES, MLA_HEADS * MLA_DV
    bat = lambda i: (i, 0, 0)
    return pl.pallas_call(
        _mla_sample_body,
        grid=(b,),
        in_specs=[pl.BlockSpec((None, t, kw), bat), pl.BlockSpec((None, s, kw), bat),
                  pl.BlockSpec((None, s, kw), bat), pl.BlockSpec((None, t, kw), bat),
                  pl.BlockSpec((None, t, kw), bat), _const_spec(vis_p.shape), _const_spec(vis_n.shape)],
        out_specs=pl.BlockSpec((None, t, vw), bat),
        out_shape=jax.ShapeDtypeStruct((b, t, vw), BF16),
        compiler_params=_params(1),
        name="mla_sample",
    )(q, k_past, v_past, k_new, v_new, vis_p, vis_n)


def _rope_tables(pos):
    posf = pos.astype(F32)
    th_r = 1.0 / (10000.0 ** jnp.linspace(0.0, 1.0, RET_DK // 2, dtype=F32))
    ang = posf[:, None] * th_r[None, :]
    c, s = jnp.cos(ang), jnp.sin(ang)
    ret = [jnp.concatenate([c] * RET_GH, 1), jnp.concatenate([s] * RET_GH, 1)]
    th_m = 1.0 / (ROPE_BASE ** (jnp.arange(0, MLA_ROPE, 2, dtype=F32) / MLA_ROPE))
    angm = posf[:, None] * th_m[None, :]
    cm, sm = jnp.cos(angm), jnp.sin(angm)
    n = pos.shape[0]
    one, z64 = jnp.ones((n, MLA_NOPE), F32), jnp.zeros((n, MLA_NOPE), F32)
    z32 = jnp.zeros((n, LANES - MLA_DQK), F32)
    mla = [jnp.concatenate([one, cm, cm, z32], 1), jnp.concatenate([z64, -sm, sm, z32], 1)]
    return jnp.concatenate(ret + mla, axis=1)


def _layer_weights(l, p):
    w_in = p["w_in"][l]
    o = RET_COLS
    w_cq, w_ckv = w_in[:, o:o + MLA_Q_LORA], w_in[:, o + MLA_Q_LORA:o + MLA_Q_LORA + MLA_KV_LORA]
    w_kr = w_in[:, o + MLA_Q_LORA + MLA_KV_LORA:o + MLA_COLS]
    o += MLA_COLS
    w_z, w_xbc, w_dt = w_in[:, o:o + SSM_D_INNER], w_in[:, o + SSM_D_INNER:o + SSM_D_INNER + CONV_DIM], w_in[:, o + SSM_D_INNER + CONV_DIM:]
    zc = lambda n: jnp.zeros((D_MODEL, n), F32)
    half = MLA_ROPE // 2
    w_kr_sw = jnp.concatenate([w_kr[:, half:], w_kr[:, :half]], axis=1)
    small_cols = lambda kr: [w_dt, zc(KROPE_LO - DT_LANES), kr, zc(LANES - KROPE_LO - MLA_ROPE)]
    w_small = jnp.concatenate(small_cols(w_kr) + [zc(KROPE_LO), w_kr_sw, zc(LANES - KROPE_LO - MLA_ROPE)], axis=1)
    swap = _swap_lanes()
    w_uq = jnp.pad(p["mla_w_uq"][l].reshape(MLA_Q_LORA, MLA_HEADS, MLA_DQK), ((0, 0), (0, 0), (0, LANES - MLA_DQK)))
    rot = (np.arange(LANES) >= KROPE_LO) & (np.arange(LANES) < KROPE_LO + MLA_ROPE)
    w_uq_sw = jnp.where(jnp.asarray(rot), w_uq[:, :, swap], 0.0)
    w_uq = jnp.concatenate([w_uq.reshape(MLA_Q_LORA, -1), w_uq_sw.reshape(MLA_Q_LORA, -1)], axis=1)
    w_ukv = p["mla_w_ukv"][l].reshape(MLA_KV_LORA, MLA_HEADS, MLA_NOPE + MLA_DV)
    w_kn = jnp.pad(w_ukv[:, :, :MLA_NOPE], ((0, 0), (0, 0), (0, LANES - MLA_NOPE)))
    w_vp = jnp.pad(w_ukv[:, :, MLA_NOPE:], ((0, 0), (0, 0), (0, LANES - MLA_DV)))
    w_kv = jnp.concatenate([w_kn.reshape(MLA_KV_LORA, -1), w_vp.reshape(MLA_KV_LORA, -1)], axis=1)

    def pad_gain(g):
        gp = jnp.pad(g, (0, LANES - MLA_DQK))
        return jnp.stack([gp, gp[swap]])

    pad16 = lambda v: jnp.pad(v, (0, LANES - DT_LANES)).reshape(1, LANES)
    w_out = p["w_out"][l]
    d_ret, d_mla = RET_HEADS * RET_DV, MLA_HEADS * MLA_DV
    hk = RET_HEADS * RET_DK

    def group_halves(wq):
        w5 = wq.reshape(D_MODEL, RET_GROUPS, RET_GH, 2, RET_DK // 2)
        return w5.transpose(0, 1, 3, 2, 4).reshape(D_MODEL, hk)

    w_ret = jnp.concatenate([group_halves(w_in[:, :hk]), group_halves(w_in[:, hk:2 * hk]), w_in[:, 2 * hk:RET_COLS]], axis=1)
    out = {
        "mix_norm": p["mix_norm"][l].reshape(1, -1), "w_ret": _bf(w_ret), "w_cq": _bf(w_cq),
        "w_ckv": _bf(w_ckv), "w_z": _bf(w_z), "w_xbc": _bf(w_xbc), "w_small": _bf(w_small),
        "q_norm": p["mla_q_norm"][l].reshape(1, -1), "kv_norm": p["mla_kv_norm"][l].reshape(1, -1),
        "w_uq": _bf(w_uq), "q_gain": pad_gain(p["mla_q_gain"][l]),
        "w_kv": _bf(w_kv), "k_gain": pad_gain(p["mla_k_gain"][l]),
        "ret_norm": p["ret_norm"][l].reshape(1, -1),
        "conv_w": p["ssm_conv_w"][l], "conv_b": p["ssm_conv_b"][l].reshape(1, -1),
        "dt_bias": pad16(p["ssm_dt_bias"][l]), "a_log": pad16(p["ssm_a_log"][l]),
        "d_skip": jnp.repeat(p["ssm_d"][l], SSM_HEADDIM).reshape(1, -1), "ssm_norm": p["ssm_norm"][l].reshape(1, -1),
        "wo_ret": _bf(w_out[:d_ret]), "wo_mla": _bf(w_out[d_ret:d_ret + d_mla]), "wo_ssm": _bf(w_out[d_ret + d_mla:]),
    }
    for tag in ("ffn1", "ffn2"):
        out[tag + "_norm"] = p[tag + "_norm"][l].reshape(1, -1)
        out[tag + "_wg"] = _bf(p[tag + "_wgu"][l][:, :D_FF])
        out[tag + "_wu"] = _bf(p[tag + "_wgu"][l][:, D_FF:])
        out[tag + "_wd"] = _bf(p[tag + "_wd"][l])
    return out


def _pack_ret_state(s):
    b = s.shape[0]
    s6 = s.reshape(b, RET_GROUPS, RET_GH, 2, RET_DK // 2, RET_DV).transpose(0, 1, 3, 2, 4, 5)
    eye = jnp.eye(RET_GH, dtype=s.dtype)
    sp = s6[:, :, :, :, :, None, :] * eye[None, None, None, :, None, :, None]
    return sp.reshape(b, RET_GROUPS, RET_GW, RET_GW)


def _unpack_ret_state(sp):
    b = sp.shape[0]
    s7 = sp.reshape(b, RET_GROUPS, 2, RET_GH, RET_DK // 2, RET_GH, RET_DV)
    d = jnp.stack([s7[:, :, :, hh, :, hh, :] for hh in range(RET_GH)], axis=2)
    return d.reshape(b, RET_HEADS, RET_DK, RET_DV)


def _layer(x, tab, w, *, ret_s0, ssm_h0, conv_buf, past, cfg):
    b, t, _ = x.shape
    n = b * t
    x1 = _ffn_call(x.reshape(n, D_MODEL), [], w["ffn1_norm"], w["ffn1_wg"], w["ffn1_wu"], w["ffn1_wd"], tm=cfg["tm_ffn"])
    if conv_buf is None:
        conv0 = jnp.zeros((b, CONV_PAD, CONV_DIM), F32)
    else:
        conv0 = jnp.concatenate([jnp.zeros((b, CONV_PAD - (CONV_W - 1), CONV_DIM), F32), conv_buf], axis=1)
    conv_in_proj = cfg["conv_in_proj"]
    outs = _inproj_call(x1, tab, w, conv0 if conv_in_proj else None, tm=cfg["tm_in"], t_per_batch=t)
    rq, rk, rv, rg, mq, ckv, small, z, xbc, kk, vv = outs[:11]
    r3 = lambda a: a.reshape(b, t, a.shape[-1])
    o_ret, s_ret = _ret_call(r3(rq), r3(rk), r3(rv), r3(rg), w["ret_norm"],
                             None if ret_s0 is None else _pack_ret_state(ret_s0), c=cfg["c_ret"], nsub=cfg["n_ret"])
    h0 = None if ssm_h0 is None else ssm_h0.reshape(b, SSM_GROUPS, -1, SSM_STATE)
    o_ssm, h_ssm = _ssd_call(r3(z), r3(xbc), r3(small), w, conv0, h0, c=cfg["c_ssd"], nsub=cfg["n_ssd"],
                             pre_act=conv_in_proj)
    if past is None:
        o_mla = _mla_prompt_call(r3(mq), r3(kk), r3(vv), tq=cfg["tq"])
    else:
        k_past, v_past, q_pos, past_pos = past
        o_mla = _mla_sample_call(r3(mq), k_past, v_past, r3(kk), r3(vv), q_pos, past_pos)
    f2 = lambda a: a.reshape(n, a.shape[-1])
    x3 = _ffn_call(x1, [(f2(o_ret), w["wo_ret"]), (f2(o_mla), w["wo_mla"]), (f2(o_ssm), w["wo_ssm"])],
                   w["ffn2_norm"], w["ffn2_wg"], w["ffn2_wu"], w["ffn2_wd"], tm=cfg["tm_ffn"])
    if conv_in_proj:
        conv_new = outs[11][:, CONV_PAD - (CONV_W - 1):]
    else:
        conv_new = jnp.concatenate([conv0[:, CONV_PAD - (CONV_W - 1):], r3(xbc)], axis=1)[:, t:]
    state = (r3(ckv), r3(small)[:, :, KROPE_LO:KROPE_LO + MLA_ROPE], _unpack_ret_state(s_ret),
             h_ssm.reshape(b, SSM_HEADS, SSM_HEADDIM, SSM_STATE), conv_new)
    return x3.reshape(b, t, D_MODEL), state


PROMPT_CFG = dict(tm_ffn=512, tm_in=256, c_ret=128, n_ret=4, c_ssd=128, n_ssd=2, tq=256, conv_in_proj=False)
SAMPLE_CFG = dict(tm_ffn=256, tm_in=256, c_ret=128, n_ret=1, c_ssd=128, n_ssd=1, tq=256, conv_in_proj=False)


def kernel(x_prompt, x_sample, cache_mla_ckv, cache_mla_krope, state_ret, state_ssm, state_conv, ffn1_norm, ffn1_wgu, ffn1_wd, mix_norm, w_in, ret_norm, mla_q_norm, mla_kv_norm, mla_w_uq, mla_w_ukv, mla_q_gain, mla_k_gain, ssm_conv_w, ssm_conv_b, ssm_dt_bias, ssm_a_log, ssm_d, ssm_norm, w_out, ffn2_norm, ffn2_wgu, ffn2_wd):
    p = dict(ffn1_norm=ffn1_norm, ffn1_wgu=ffn1_wgu, ffn1_wd=ffn1_wd, mix_norm=mix_norm, w_in=w_in, ret_norm=ret_norm,
             mla_q_norm=mla_q_norm, mla_kv_norm=mla_kv_norm, mla_w_uq=mla_w_uq, mla_w_ukv=mla_w_ukv,
             mla_q_gain=mla_q_gain, mla_k_gain=mla_k_gain, ssm_conv_w=ssm_conv_w, ssm_conv_b=ssm_conv_b,
             ssm_dt_bias=ssm_dt_bias, ssm_a_log=ssm_a_log, ssm_d=ssm_d, ssm_norm=ssm_norm, w_out=w_out,
             ffn2_norm=ffn2_norm, ffn2_wgu=ffn2_wgu, ffn2_wd=ffn2_wd)
    depth = ffn1_norm.shape[0]
    b_s, t_s, _ = x_sample.shape
    t_p = x_prompt.shape[1]
    past = cache_mla_ckv.shape[2]
    q_pos_s = past + np.arange(t_s)
    past_pos = np.arange(past)
    tab_p = _rope_tables(jnp.arange(t_p, dtype=jnp.int32))
    tab_s = jnp.tile(_rope_tables(jnp.asarray(q_pos_s, jnp.int32)), (b_s, 1))
    tab_past = _rope_tables(jnp.arange(past, dtype=jnp.int32))[:, 2 * LANES:]
    yp, ys = x_prompt, x_sample
    new_p, new_s = [[] for _ in range(5)], [[] for _ in range(5)]
    for l in range(depth):
        w = _layer_weights(l, p)
        yp, st_p = _layer(yp, tab_p, w, ret_s0=None, ssm_h0=None, conv_buf=None, past=None, cfg=PROMPT_CFG)
        k_past, v_past = _kvprep_call(cache_mla_ckv[l], cache_mla_krope[l], w["w_kv"], w["k_gain"], tab_past, tm=512)
        ys, st_s = _layer(ys, tab_s, w, ret_s0=state_ret[l], ssm_h0=state_ssm[l], conv_buf=state_conv[l],
                          past=(k_past, v_past, q_pos_s, past_pos), cfg=SAMPLE_CFG)
        for i in range(5):
            new_p[i].append(st_p[i])
            new_s[i].append(st_s[i])
    return (yp, ys) + tuple(jnp.stack(a) for a in new_p) + tuple(jnp.stack(a) for a in new_s)
```

```python
import functools

import numpy as np
import jax
import jax.numpy as jnp
from jax import lax
from jax.experimental import pallas as pl
from jax.experimental.pallas import tpu as pltpu

D_MODEL = 1024
D_FF = 2816
CHUNK = 64
EPS = 1e-6
RET_HEADS = 8
RET_DK = 64
RET_DV = 64
MLA_HEADS = 8
MLA_Q_LORA = 384
MLA_KV_LORA = 256
MLA_NOPE = 64
MLA_ROPE = 32
MLA_DQK = MLA_NOPE + MLA_ROPE
MLA_DV = 64
ROPE_BASE = 10000.0
SSM_HEADS = 16
SSM_HEADDIM = 64
SSM_D_INNER = SSM_HEADS * SSM_HEADDIM
SSM_GROUPS = 2
SSM_STATE = 128
CONV_W = 4
CONV_DIM = SSM_D_INNER + 2 * SSM_GROUPS * SSM_STATE
RET_COLS = 2 * RET_HEADS * RET_DK + 2 * RET_HEADS * RET_DV
MLA_COLS = MLA_Q_LORA + MLA_KV_LORA + MLA_ROPE

LANES = 128
HALF = LANES // 2
N_PAIRS = 4
RET_GROUPS = 2
RET_GH = RET_HEADS // RET_GROUPS
RET_GW = RET_GH * RET_DK
LOG2E = 1.4426950408889634
CONV_PAD = 8
DT_LANES = SSM_HEADS
KROPE_LO = MLA_NOPE
V7X_VMEM_LIMIT_BYTES = 56 * 1024 * 1024

F32 = jnp.float32
BF16 = jnp.bfloat16


def _bf(x):
    return x.astype(BF16)


def _dot(a, b):
    return jnp.dot(a, b, preferred_element_type=F32)


def _dot_nt(a, b):
    return lax.dot_general(a, b, (((1,), (1,)), ((), ())), preferred_element_type=F32)


def _sigmoid(x):
    return 1.0 / (1.0 + jnp.exp(-x))


def _silu(x):
    return x * _sigmoid(x)


def _rms(x):
    return x * lax.rsqrt(jnp.mean(x * x, axis=-1, keepdims=True) + EPS)


def _split3(x):
    hi = _bf(x)
    r = x - hi.astype(F32)
    mid = _bf(r)
    lo = _bf(r - mid.astype(F32))
    return hi, mid, lo


def _exact01(m01, x):
    hi, mid, lo = _split3(x)
    return _dot(m01, hi) + _dot(m01, mid) + _dot(m01, lo)


def _exact01_nt(m01, x):
    hi, mid, lo = _split3(x)
    return _dot_nt(m01, hi) + _dot_nt(m01, mid) + _dot_nt(m01, lo)


def _exact01_tn(x, m01):
    hi, mid, lo = _split3(x)
    return _dot(hi, m01) + _dot(mid, m01) + _dot(lo, m01)


def _dot_hi_mid(x, m01):
    hi = _bf(x)
    return _dot(hi, m01) + _dot(_bf(x - hi.astype(F32)), m01)


def _lane_lo(rows):
    return lax.broadcasted_iota(jnp.int32, (rows, LANES), 1) < HALF


def _conv_silu_slab(xbuf, j, x_new, cw, cb):
    rows = x_new.shape[0]
    xbuf[j, CONV_PAD:CONV_PAD + rows, :] = x_new
    conv = cb
    for wi in range(CONV_W):
        off = CONV_PAD - (CONV_W - 1) + wi
        conv = conv + cw[wi:wi + 1, :] * xbuf[j, off:off + rows, :]
    xbuf[j, 0:CONV_PAD, :] = xbuf[j, rows:rows + CONV_PAD, :]
    return _silu(conv)


def _const_spec(shape):
    nd = len(shape)
    return pl.BlockSpec(shape, lambda *_: (0,) * nd, pipeline_mode=pl.Buffered(1))


def _params(n_axes):
    return pltpu.CompilerParams(dimension_semantics=("arbitrary",) * n_axes,
                                vmem_limit_bytes=V7X_VMEM_LIMIT_BYTES)


def _ffn_body(*refs, n_pre, fc):
    x_ref = refs[0]
    pre = [(refs[1 + 2 * i], refs[2 + 2 * i]) for i in range(n_pre)]
    g_ref, wg_ref, wu_ref, wd_ref, o_ref = refs[1 + 2 * n_pre:]
    x = x_ref[...]
    for m_ref, w_ref in pre:
        x = x + _dot(m_ref[...], w_ref[...])
    xn = _bf(_rms(x) * g_ref[...])
    acc = None
    for c in range(D_FF // fc):
        gate = _dot(xn, wg_ref[:, c * fc:(c + 1) * fc])
        up = _dot(xn, wu_ref[:, c * fc:(c + 1) * fc])
        part = _dot(_bf(_silu(gate) * up), wd_ref[c * fc:(c + 1) * fc, :])
        acc = part if acc is None else acc + part
    o_ref[...] = x + 0.5 * acc


def _ffn_call(x, pre, gain, wg, wu, wd, *, tm, fc=256):
    n = x.shape[0]
    tm = min(tm, n)
    assert n % tm == 0 and D_FF % fc == 0
    args, specs = [x], [pl.BlockSpec((tm, D_MODEL), lambda i: (i, 0))]
    for m, w in pre:
        args += [m, w]
        specs += [pl.BlockSpec((tm, m.shape[1]), lambda i: (i, 0)), _const_spec(w.shape)]
    args += [gain, wg, wu, wd]
    specs += [_const_spec(gain.shape), _const_spec(wg.shape), _const_spec(wu.shape), _const_spec(wd.shape)]
    return pl.pallas_call(
        functools.partial(_ffn_body, n_pre=len(pre), fc=fc),
        grid=(n // tm,),
        in_specs=specs,
        out_specs=pl.BlockSpec((tm, D_MODEL), lambda i: (i, 0)),
        out_shape=jax.ShapeDtypeStruct((n, D_MODEL), F32),
        compiler_params=_params(1),
        name="ffn",
    )(*args)


def _kv_from_latent(ckv_n, krope_slab, krope_sw_slab, wkv_ref, kgain, kgain_sw, cos, sin_s, k_ref, v_ref):
    kvp = _dot(_bf(ckv_n), wkv_ref[...])
    ones_hi = jnp.where(_lane_lo(kvp.shape[0]), 0.0, 1.0)
    g_cos = kgain * cos
    rot = krope_sw_slab * (kgain_sw * sin_s)
    for h in range(MLA_HEADS):
        kh = kvp[:, h * LANES:(h + 1) * LANES] + krope_slab
        ss = jnp.sum(kh * kh, axis=-1, keepdims=True) * (1.0 / MLA_DQK)
        k_ref[:, h * LANES:(h + 1) * LANES] = _bf((kh * g_cos + rot) * lax.rsqrt(ss + EPS))
        vs = slice((MLA_HEADS + h) * LANES, (MLA_HEADS + h + 1) * LANES)
        v_ref[:, h * LANES:(h + 1) * LANES] = _bf(kvp[:, vs] + ones_hi)


def _kvprep_body(ckv_ref, kr_ref, place_ref, wkv_ref, kgain_ref, tab_ref, k_ref, v_ref):
    slabs = _exact01_tn(kr_ref[...], place_ref[...])
    tab = tab_ref[...]
    kgain = kgain_ref[...]
    _kv_from_latent(ckv_ref[...], slabs[:, :LANES], slabs[:, LANES:], wkv_ref, kgain[0:1], kgain[1:2],
                    tab[:, :LANES], tab[:, LANES:], k_ref, v_ref)


def _swap_lanes():
    idx = np.arange(LANES)
    half = MLA_ROPE // 2
    idx[KROPE_LO:KROPE_LO + half] += half
    idx[KROPE_LO + half:KROPE_LO + MLA_ROPE] -= half
    return idx


def _kvprep_call(ckv, krope, layer, wkv, kgain, tab, *, tm):
    _, b, s, _ = ckv.shape
    tm = min(tm, s)
    assert s % tm == 0
    place = np.zeros((MLA_ROPE, 2 * LANES), np.float32)
    place[np.arange(MLA_ROPE), KROPE_LO + np.arange(MLA_ROPE)] = 1.0
    place[np.arange(MLA_ROPE), LANES + _swap_lanes()[KROPE_LO:KROPE_LO + MLA_ROPE]] = 1.0
    place = jnp.asarray(place, BF16)
    return pl.pallas_call(
        _kvprep_body,
        grid=(b, s // tm),
        in_specs=[pl.BlockSpec((None, None, tm, MLA_KV_LORA), lambda i, j: (layer, i, j, 0)),
                  pl.BlockSpec((None, None, tm, MLA_ROPE), lambda i, j: (layer, i, j, 0)),
                  _const_spec(place.shape), _const_spec(wkv.shape), _const_spec(kgain.shape),
                  pl.BlockSpec((tm, 2 * LANES), lambda i, j: (j, 0))],
        out_specs=[pl.BlockSpec((None, tm, MLA_HEADS * LANES), lambda i, j: (i, j, 0))] * 2,
        out_shape=[jax.ShapeDtypeStruct((b, s, MLA_HEADS * LANES), BF16)] * 2,
        compiler_params=_params(2),
        name="kvprep",
    )(ckv, krope, place, wkv, kgain, tab)


def _inproj_body(x_ref, tab_ref, gmix_ref, wret_ref, wcq_ref, wckv_ref, wz_ref, wxbc_ref, wsm_ref,
                 gq_ref, gkv_ref, wuq_ref, qgain_ref, wkv_ref, kgain_ref,
                 rq_ref, rk_ref, rv_ref, rg_ref, mq_ref, ckv_ref, sm_ref, z_ref, xbc_ref, k_ref, v_ref):
    hn = _bf(_rms(x_ref[...]) * gmix_ref[...])
    tab = tab_ref[...]
    cos_r, sin_r = tab[:, 0:LANES], tab[:, LANES:2 * LANES]
    cos_m, sin_s = tab[:, 2 * LANES:3 * LANES], tab[:, 3 * LANES:4 * LANES]

    xbc_ref[...] = _dot(hn, wxbc_ref[...])
    z_ref[...] = _dot(hn, wz_ref[...])

    hk = RET_HEADS * RET_DK
    ret = _dot(hn, wret_ref[...])
    for out_ref, base, scale in ((rq_ref, 0, 1.0), (rk_ref, hk, RET_DK ** -0.5)):
        for g in range(RET_GROUPS):
            lo_sl = slice(g * RET_GW, g * RET_GW + LANES)
            hi_sl = slice(g * RET_GW + LANES, (g + 1) * RET_GW)
            x1 = ret[:, base + g * RET_GW:base + g * RET_GW + LANES]
            x2 = ret[:, base + g * RET_GW + LANES:base + (g + 1) * RET_GW]
            out_ref[:, lo_sl] = _bf((x1 * cos_r - x2 * sin_r) * scale)
            out_ref[:, hi_sl] = _bf((x1 * sin_r + x2 * cos_r) * scale)
    rv_ref[...] = _bf(ret[:, 2 * hk:3 * hk])
    rg_ref[...] = _silu(ret[:, 3 * hk:])

    cq = _bf(_rms(_dot(hn, wcq_ref[...])) * gq_ref[...])
    qp = _dot(cq, wuq_ref[...])
    qgain = qgain_ref[...]
    q_scale = (MLA_DQK ** -0.5) * LOG2E
    g_cos = qgain[0:1] * cos_m * q_scale
    g_sin = qgain[1:2] * sin_s * q_scale
    for h in range(MLA_HEADS):
        qh = qp[:, h * LANES:(h + 1) * LANES]
        qs = qp[:, (MLA_HEADS + h) * LANES:(MLA_HEADS + h + 1) * LANES]
        ss = jnp.sum(qh * qh, axis=-1, keepdims=True) * (1.0 / MLA_DQK)
        mq_ref[:, h * LANES:(h + 1) * LANES] = _bf((qh * g_cos + qs * g_sin) * lax.rsqrt(ss + EPS))

    small2 = _dot(hn, wsm_ref[...])
    small = small2[:, :LANES]
    sm_ref[...] = small
    ckv_n = _rms(_dot(hn, wckv_ref[...])) * gkv_ref[...]
    ckv_ref[...] = ckv_n
    lane = lax.broadcasted_iota(jnp.int32, small.shape, 1)
    rot_lanes = (lane >= KROPE_LO) & (lane < KROPE_LO + MLA_ROPE)
    kgain = kgain_ref[...]
    _kv_from_latent(ckv_n, jnp.where(rot_lanes, small, 0.0), jnp.where(rot_lanes, small2[:, LANES:], 0.0),
                    wkv_ref, kgain[0:1], kgain[1:2], cos_m, sin_s, k_ref, v_ref)


def _inproj_call(x, tab, w, *, tm, t_per_batch):
    n = x.shape[0]
    tm = min(tm, n)
    assert n % tm == 0
    if tm <= t_per_batch:
        assert t_per_batch % tm == 0 and tab.shape[0] == t_per_batch
        per = t_per_batch // tm
        tab_spec = pl.BlockSpec((tm, 4 * LANES), lambda i: (i % per, 0))
    else:
        assert tab.shape[0] == tm
        tab_spec = pl.BlockSpec((tm, 4 * LANES), lambda i: (0, 0))
    consts = [w["mix_norm"], w["w_ret"], w["w_cq"], w["w_ckv"], w["w_z"], w["w_xbc"], w["w_small"],
              w["q_norm"], w["kv_norm"], w["w_uq"], w["q_gain"], w["w_kv"], w["k_gain"]]
    widths = [(512, BF16), (512, BF16), (512, BF16), (512, F32), (MLA_HEADS * LANES, BF16),
              (MLA_KV_LORA, F32), (LANES, F32), (SSM_D_INNER, F32), (CONV_DIM, F32),
              (MLA_HEADS * LANES, BF16), (MLA_HEADS * LANES, BF16)]
    return pl.pallas_call(
        _inproj_body,
        grid=(n // tm,),
        in_specs=[pl.BlockSpec((tm, D_MODEL), lambda i: (i, 0)), tab_spec] + [_const_spec(c.shape) for c in consts],
        out_specs=[pl.BlockSpec((tm, wd), lambda i: (i, 0)) for wd, _ in widths],
        out_shape=[jax.ShapeDtypeStruct((n, wd), dt) for wd, dt in widths],
        compiler_params=_params(1),
        name="inproj",
    )(x, tab, *consts)


def _ret_body(*refs, has_s0, c, nsub):
    (q_ref, k_ref, v_ref, g_ref, dm_ref, qd_ref, kd_ref, cd_ref, bd_ref, eye_ref, gn_ref, hm_ref, vm_ref) = refs[:13]
    s0_ref = refs[13] if has_s0 else None
    o_ref, sout_ref, s_ref = refs[-3:]
    half_dk = RET_DK // 2

    def state_blocks():
        for g in range(RET_GROUPS):
            for hh in range(RET_GH):
                cols = slice(hh * RET_DV, (hh + 1) * RET_DV)
                for part in range(2):
                    r0 = part * LANES + hh * half_dk
                    yield g * RET_GH + hh, slice(part * half_dk, (part + 1) * half_dk), g, slice(r0, r0 + half_dk), cols

    @pl.when(pl.program_id(1) == 0)
    def _():
        s_ref[...] = jnp.zeros(s_ref.shape, F32)
        if has_s0:
            for h, hrows, g, rows_, cols in state_blocks():
                s_ref[g, rows_, cols] = s0_ref[h, hrows, :]

    lo = _lane_lo(c)
    eye = eye_ref[...]
    for sub in range(nsub):
        rs = slice(sub * c, (sub + 1) * c)
        for g in range(RET_GROUPS):
            gs = slice(g * RET_GW, (g + 1) * RET_GW)
            qg, kg, vg = q_ref[rs, gs], k_ref[rs, gs], v_ref[rs, gs]
            state = s_ref[g]
            o = _dot(qg, _bf(state)) * qd_ref[:, gs]
            for hh in range(RET_GH):
                att = _dot_nt(qg * hm_ref[hh], kg) * dm_ref[g * RET_GH + hh]
                o = o + _dot(_bf(att), vg * vm_ref[hh])
            for half in range(2):
                sl = slice(g * RET_GW + half * LANES, g * RET_GW + (half + 1) * LANES)
                oh = o[:, half * LANES:(half + 1) * LANES]
                oo = oh * oh
                s_lo = jnp.sum(jnp.where(lo, oo, 0.0), axis=-1, keepdims=True)
                s_hi = jnp.sum(jnp.where(lo, 0.0, oo), axis=-1, keepdims=True)
                ss = jnp.where(lo, s_lo, s_hi) * (1.0 / RET_DV)
                o_ref[rs, sl] = _bf(oh * lax.rsqrt(ss + EPS) * gn_ref[:, sl] * g_ref[rs, sl])
            kdec = _bf(kg.astype(F32) * kd_ref[:, gs])
            kdec_t = _bf(_dot_nt(eye, kdec))
            s_ref[g] = state * cd_ref[g] + _dot(kdec_t, vg) * bd_ref[...]

    @pl.when(pl.program_id(1) == pl.num_programs(1) - 1)
    def _():
        for h, hrows, g, rows_, cols in state_blocks():
            sout_ref[h, hrows, :] = s_ref[g, rows_, cols]


def _ret_qk_head(width):
    return (np.arange(width) % LANES) // (RET_DK // 2) + RET_GH * (np.arange(width) // RET_GW)


def _ret_tables(c):
    log_g = jnp.log(1.0 - 2.0 ** (-5.0 - jnp.arange(RET_HEADS, dtype=F32)))
    idx = jnp.arange(c, dtype=F32)
    diff = idx[:, None] - idx[None, :]
    dmask = jnp.where(diff >= 0, jnp.exp(log_g[:, None, None] * jnp.maximum(diff, 0.0)), 0.0)
    q_dec = jnp.exp(log_g[:, None] * (idx[None, :] + 1.0))
    k_dec = jnp.exp(log_g[:, None] * (c - 1.0 - idx[None, :]))
    c_dec = jnp.exp(log_g * c)
    w = RET_HEADS * RET_DK
    qk_head = _ret_qk_head(w)
    v_head = np.arange(w) // RET_DV
    qd = q_dec.T[:, v_head]
    kd = k_dec.T[:, qk_head]
    cd = jnp.broadcast_to(c_dec[qk_head].reshape(RET_GROUPS, RET_GW, 1), (RET_GROUPS, RET_GW, RET_GW))
    bd = jnp.asarray((qk_head[:RET_GW, None] == v_head[None, :RET_GW]).astype(np.float32))
    hm = np.stack([np.broadcast_to(qk_head[:RET_GW] == hh, (c, RET_GW)) for hh in range(RET_GH)])
    vm = np.stack([np.broadcast_to(v_head[:RET_GW] == hh, (c, RET_GW)) for hh in range(RET_GH)])
    return dmask, qd, kd, cd, bd, jnp.asarray(hm, BF16), jnp.asarray(vm, BF16)


def _ret_call(rq, rk, rv, rg, gn, s0, *, c, nsub):
    b, t, w = rq.shape
    c = min(c, t)
    nsub = min(nsub, t // c)
    blk = c * nsub
    assert t % blk == 0
    dmask, qd, kd, cd, bd, hm, vm = _ret_tables(c)
    eye = jnp.eye(RET_GW, dtype=BF16)
    tok = lambda i, j: (i, j, 0)
    args = [rq, rk, rv, rg, dmask, qd, kd, cd, bd, eye, gn, hm, vm]
    specs = [pl.BlockSpec((None, blk, w), tok)] * 4 + [_const_spec(a.shape) for a in args[4:]]
    st_spec = pl.BlockSpec((None, RET_HEADS, RET_DK, RET_DV), lambda i, j: (i, 0, 0, 0))
    if s0 is not None:
        args.append(s0)
        specs.append(st_spec)
    return pl.pallas_call(
        functools.partial(_ret_body, has_s0=s0 is not None, c=c, nsub=nsub),
        grid=(b, t // blk),
        in_specs=specs,
        out_specs=[pl.BlockSpec((None, blk, w), tok), st_spec],
        out_shape=[jax.ShapeDtypeStruct((b, t, w), BF16),
                   jax.ShapeDtypeStruct((b, RET_HEADS, RET_DK, RET_DV), F32)],
        scratch_shapes=[pltpu.VMEM((RET_GROUPS, RET_GW, RET_GW), F32)],
        compiler_params=_params(2),
        name="retention",
    )(*args)


def _ssd_body(*refs, has_h0, c, nsub):
    (z_ref, xbc_ref, sm_ref, cw_ref, cb_ref, dtb_ref, alog_ref, dsk_ref, ng_ref,
     tri_ref, eye_ref, spread_ref, conv0_ref) = refs[:13]
    h0_ref = refs[13] if has_h0 else None
    o_ref, h_ref, xbuf, st = refs[-4:]
    rows = z_ref.shape[0]
    ci = pl.program_id(1)

    n_slabs = CONV_DIM // LANES

    @pl.when(ci == 0)
    def _():
        for j in range(n_slabs):
            xbuf[j, 0:CONV_PAD, :] = conv0_ref[:, j * LANES:(j + 1) * LANES]
        for g in range(SSM_GROUPS):
            st[g] = h0_ref[g].T if has_h0 else jnp.zeros(st.shape[1:], F32)

    act_slabs = []
    for j in range(n_slabs):
        sl = slice(j * LANES, (j + 1) * LANES)
        act_slabs.append(_conv_silu_slab(xbuf, j, xbc_ref[:, sl], cw_ref[:, sl], cb_ref[:, sl]))
    x_slab0, b_slab0, c_slab0 = 0, SSM_D_INNER // LANES, SSM_D_INNER // LANES + SSM_GROUPS

    lane = lax.broadcasted_iota(jnp.int32, (rows, LANES), 1)
    lane1 = lax.broadcasted_iota(jnp.int32, (1, LANES), 1)
    dt_raw = jnp.where(lane < DT_LANES, sm_ref[...], 0.0) + dtb_ref[...]
    dt_all = jnp.maximum(dt_raw, 0.0) + jnp.log1p(jnp.exp(-jnp.abs(dt_raw)))
    a_neg = jnp.where(lane1 < DT_LANES, -jnp.exp(alog_ref[...]), 0.0)
    dta_all = dt_all * a_neg
    ldt_all = jnp.maximum(jnp.log(dt_all), -1e30)
    eye = eye_ref[...]
    lo = _lane_lo(c)
    lo1 = lane1 < HALF
    m_lo = _bf(jnp.where(lo, 1.0, 0.0))
    m_hi = _bf(jnp.where(lo, 0.0, 1.0))
    r_i = lax.broadcasted_iota(jnp.int32, (c, c), 0)
    c_i = lax.broadcasted_iota(jnp.int32, (c, c), 1)
    causal = r_i >= c_i

    def pair_lanes(v, h):
        return jnp.where(lo1, v[:, h:h + 1], v[:, h + 1:h + 2])

    for sub in range(nsub):
        r0 = sub * c
        rs_ = slice(r0, r0 + c)
        dtv = dt_all[r0:r0 + c]
        a_cum = _exact01(tri_ref[...], dta_all[r0:r0 + c])
        a_col = a_cum * LOG2E
        row_t = _exact01_nt(eye, (a_cum - ldt_all[r0:r0 + c]) * LOG2E)
        a_last = a_cum[c - 1:c, :]
        e_last = jnp.exp(a_last)
        exp_a = _dot_hi_mid(jnp.exp(a_cum), spread_ref[...])
        w_dt = _dot_hi_mid(jnp.exp(a_last - a_cum) * dtv, spread_ref[...])
        for g in range(SSM_GROUPS):
            bmb = _bf(act_slabs[b_slab0 + g][r0:r0 + c])
            cmb = _bf(act_slabs[c_slab0 + g][r0:r0 + c])
            scores = _dot_nt(cmb, bmb)
            y_off = _dot(cmb, _bf(st[g]))
            ssq = jnp.zeros((c, 1), F32)
            gated, xw, e_row = [], [], []
            for jj in range(N_PAIRS):
                j = g * N_PAIRS + jj
                sl = slice(j * LANES, (j + 1) * LANES)
                h0 = 2 * j
                xsl = act_slabs[x_slab0 + j][r0:r0 + c]
                xb = _bf(xsl)
                y = y_off[:, jj * LANES:(jj + 1) * LANES] * exp_a[:, sl] + dsk_ref[:, sl] * xsl
                for e, msk in enumerate((m_lo, m_hi)):
                    h = h0 + e
                    seg = jnp.where(causal, a_col[:, h:h + 1] - row_t[h:h + 1, :], -jnp.inf)
                    y = y + _dot(_bf(scores * jnp.exp2(seg)), xb * msk)
                zs = z_ref[rs_, sl]
                gt = y * _silu(zs)
                ssq = ssq + jnp.sum(gt * gt, axis=-1, keepdims=True)
                gated.append(gt)
                xw.append(_bf(xsl * w_dt[:, sl]))
                e_row.append(pair_lanes(e_last, h0))
            rs = lax.rsqrt(ssq * (1.0 / (SSM_D_INNER // SSM_GROUPS)) + EPS)
            for jj in range(N_PAIRS):
                sl = slice((g * N_PAIRS + jj) * LANES, (g * N_PAIRS + jj + 1) * LANES)
                o_ref[rs_, sl] = _bf(gated[jj] * rs * ng_ref[:, sl])
            bm_t = _bf(_dot_nt(eye, bmb))
            upd = _dot(bm_t, jnp.concatenate(xw, axis=1))
            st[g] = st[g] * jnp.concatenate(e_row, axis=1) + upd

    @pl.when(ci == pl.num_programs(1) - 1)
    def _():
        for g in range(SSM_GROUPS):
            h_ref[g] = st[g].T


def _ssd_call(z, xbc, small, w, conv0, h0, *, c, nsub):
    b, t, _ = z.shape
    c = min(c, t)
    nsub = min(nsub, t // c)
    blk = c * nsub
    assert t % blk == 0 and c >= CONV_PAD
    tri = jnp.asarray(np.tril(np.ones((c, c), np.float32)), BF16)
    eye = jnp.eye(LANES, dtype=BF16)
    spread = np.zeros((LANES, SSM_D_INNER), np.float32)
    spread[np.arange(SSM_D_INNER) // SSM_HEADDIM, np.arange(SSM_D_INNER)] = 1.0
    spread = jnp.asarray(spread, BF16)
    gw = SSM_D_INNER // SSM_GROUPS
    tok = lambda i, j: (i, j, 0)
    consts = [w["conv_w"], w["conv_b"], w["dt_bias"], w["a_log"], w["d_skip"], w["ssm_norm"], tri, eye, spread]
    args = [z, xbc, small] + consts + [conv0]
    specs = ([pl.BlockSpec((None, blk, SSM_D_INNER), tok), pl.BlockSpec((None, blk, CONV_DIM), tok),
              pl.BlockSpec((None, blk, LANES), tok)] + [_const_spec(a.shape) for a in consts]
             + [pl.BlockSpec((None, CONV_PAD, CONV_DIM), lambda i, j: (i, 0, 0))])
    if h0 is not None:
        args.append(h0)
        specs.append(pl.BlockSpec((None, SSM_GROUPS, gw, SSM_STATE), lambda i, j: (i, 0, 0, 0)))
    return pl.pallas_call(
        functools.partial(_ssd_body, has_h0=h0 is not None, c=c, nsub=nsub),
        grid=(b, t // blk),
        in_specs=specs,
        out_specs=[pl.BlockSpec((None, blk, SSM_D_INNER), tok),
                   pl.BlockSpec((None, SSM_GROUPS, gw, SSM_STATE), lambda i, j: (i, 0, 0, 0))],
        out_shape=[jax.ShapeDtypeStruct((b, t, SSM_D_INNER), BF16),
                   jax.ShapeDtypeStruct((b, SSM_GROUPS, gw, SSM_STATE), F32)],
        scratch_shapes=[pltpu.VMEM((CONV_DIM // LANES, CONV_PAD + blk, LANES), F32),
                        pltpu.VMEM((SSM_GROUPS, SSM_STATE, gw), F32)],
        compiler_params=_params(2),
        name="ssd",
    )(*args)


def _pair_output(acc0, acc1, lo):
    return jnp.where(lo, acc0 / pltpu.roll(acc0, HALF, 1), pltpu.roll(acc1, HALF, 1) / acc1)


def _mla_prompt_body(q_ref, k_ref, v_ref, o_ref, m_sc, acc_sc, *, tq):
    qi = pl.program_id(1)
    m_sc[...] = jnp.full(m_sc.shape, -jnp.inf, F32)
    acc_sc[...] = jnp.zeros(acc_sc.shape, F32)
    r_i = lax.broadcasted_iota(jnp.int32, (tq, tq), 0)
    c_i = lax.broadcasted_iota(jnp.int32, (tq, tq), 1)
    visible = lax.shift_right_logical(c_i, 6) <= lax.shift_right_logical(r_i, 6)

    def step(t, diagonal):
        off = pl.multiple_of(t * tq, tq)
        for h in range(MLA_HEADS):
            hs = slice(h * LANES, (h + 1) * LANES)
            s = _dot_nt(q_ref[:, hs], k_ref[pl.ds(off, tq), hs])
            if diagonal:
                s = jnp.where(visible, s, -jnp.inf)
            m_old = m_sc[h]
            m_new = jnp.maximum(m_old, jnp.max(s, axis=-1, keepdims=True))
            alpha = jnp.exp2(m_old - m_new)
            p = jnp.exp2(s - jnp.concatenate([m_new] * (tq // LANES), axis=1))
            acc_sc[h] = acc_sc[h] * alpha + _dot(_bf(p), v_ref[pl.ds(off, tq), hs])
            m_sc[h] = m_new

    def body(t, carry):
        step(t, False)
        return carry

    lax.fori_loop(0, qi, body, 0)
    step(qi, True)
    lo = _lane_lo(tq)
    for j in range(N_PAIRS):
        o_ref[:, j * LANES:(j + 1) * LANES] = _bf(_pair_output(acc_sc[2 * j], acc_sc[2 * j + 1], lo))


def _mla_prompt_call(q, k, v, *, tq):
    b, t, _ = q.shape
    tq = min(tq, t)
    assert t % tq == 0 and tq % LANES == 0 and CHUNK == 64
    kw = MLA_HEADS * LANES
    return pl.pallas_call(
        functools.partial(_mla_prompt_body, tq=tq),
        grid=(b, t // tq),
        in_specs=[pl.BlockSpec((None, tq, kw), lambda i, j: (i, j, 0)),
                  pl.BlockSpec((None, t, kw), lambda i, j: (i, 0, 0)),
                  pl.BlockSpec((None, t, kw), lambda i, j: (i, 0, 0))],
        out_specs=pl.BlockSpec((None, tq, MLA_HEADS * MLA_DV), lambda i, j: (i, j, 0)),
        out_shape=jax.ShapeDtypeStruct((b, t, MLA_HEADS * MLA_DV), BF16),
        scratch_shapes=[pltpu.VMEM((MLA_HEADS, tq, LANES), F32), pltpu.VMEM((MLA_HEADS, tq, LANES), F32)],
        compiler_params=_params(2),
        name="mla_prompt",
    )(q, k, v)


def _mla_sample_body(q_ref, kp_ref, vp_ref, kn_ref, vn_ref, visp_ref, visn_ref, o_ref):
    lo = _lane_lo(q_ref.shape[0])
    vis_p = visp_ref[...] > 0.5
    vis_n = visn_ref[...] > 0.5
    for j in range(N_PAIRS):
        accs = []
        for h in (2 * j, 2 * j + 1):
            hs = slice(h * LANES, (h + 1) * LANES)
            qh = q_ref[:, hs]
            s_p = jnp.where(vis_p, _dot_nt(qh, kp_ref[:, hs]), -jnp.inf)
            s_n = jnp.where(vis_n, _dot_nt(qh, kn_ref[:, hs]), -jnp.inf)
            m = jnp.maximum(jnp.max(s_p, axis=-1, keepdims=True), jnp.max(s_n, axis=-1, keepdims=True))
            accs.append(_dot(_bf(jnp.exp2(s_p - m)), vp_ref[:, hs]) + _dot(_bf(jnp.exp2(s_n - m)), vn_ref[:, hs]))
        o_ref[:, j * LANES:(j + 1) * LANES] = _bf(_pair_output(accs[0], accs[1], lo))


def _mla_sample_call(q, k_past, v_past, k_new, v_new, q_pos, past_pos):
    b, t, _ = q.shape
    s = k_past.shape[1]
    vis_p = jnp.asarray((past_pos[None, :] // CHUNK <= q_pos[:, None] // CHUNK).astype(np.float32))
    vis_n = jnp.asarray((q_pos[None, :] // CHUNK <= q_pos[:, None] // CHUNK).astype(np.float32))
    kw, vw = MLA_HEADS * LANES, MLA_HEADS * MLA_DV
    bat = lambda i: (i, 0, 0)
    return pl.pallas_call(
        _mla_sample_body,
        grid=(b,),
        in_specs=[pl.BlockSpec((None, t, kw), bat), pl.BlockSpec((None, s, kw), bat),
                  pl.BlockSpec((None, s, kw), bat), pl.BlockSpec((None, t, kw), bat),
                  pl.BlockSpec((None, t, kw), bat), _const_spec(vis_p.shape), _const_spec(vis_n.shape)],
        out_specs=pl.BlockSpec((None, t, vw), bat),
        out_shape=jax.ShapeDtypeStruct((b, t, vw), BF16),
        compiler_params=_params(1),
        name="mla_sample",
    )(q, k_past, v_past, k_new, v_new, vis_p, vis_n)


def _rope_tables(pos):
    posf = pos.astype(F32)
    th_r = 1.0 / (10000.0 ** jnp.linspace(0.0, 1.0, RET_DK // 2, dtype=F32))
    ang = posf[:, None] * th_r[None, :]
    c, s = jnp.cos(ang), jnp.sin(ang)
    ret = [jnp.concatenate([c] * RET_GH, 1), jnp.concatenate([s] * RET_GH, 1)]
    th_m = 1.0 / (ROPE_BASE ** (jnp.arange(0, MLA_ROPE, 2, dtype=F32) / MLA_ROPE))
    angm = posf[:, None] * th_m[None, :]
    cm, sm = jnp.cos(angm), jnp.sin(angm)
    n = pos.shape[0]
    one, z64 = jnp.ones((n, MLA_NOPE), F32), jnp.zeros((n, MLA_NOPE), F32)
    z32 = jnp.zeros((n, LANES - MLA_DQK), F32)
    mla = [jnp.concatenate([one, cm, cm, z32], 1), jnp.concatenate([z64, -sm, sm, z32], 1)]
    return jnp.concatenate(ret + mla, axis=1)


def _layer_weights(l, p):
    w_in = _bf(p["w_in"][l])
    o = RET_COLS
    w_cq, w_ckv = w_in[:, o:o + MLA_Q_LORA], w_in[:, o + MLA_Q_LORA:o + MLA_Q_LORA + MLA_KV_LORA]
    w_kr = w_in[:, o + MLA_Q_LORA + MLA_KV_LORA:o + MLA_COLS]
    o += MLA_COLS
    w_z, w_xbc, w_dt = w_in[:, o:o + SSM_D_INNER], w_in[:, o + SSM_D_INNER:o + SSM_D_INNER + CONV_DIM], w_in[:, o + SSM_D_INNER + CONV_DIM:]
    zc = lambda n: jnp.zeros((D_MODEL, n), BF16)
    half = MLA_ROPE // 2
    w_kr_sw = jnp.concatenate([w_kr[:, half:], w_kr[:, :half]], axis=1)
    small_cols = lambda kr: [w_dt, zc(KROPE_LO - DT_LANES), kr, zc(LANES - KROPE_LO - MLA_ROPE)]
    w_small = jnp.concatenate(small_cols(w_kr) + [zc(KROPE_LO), w_kr_sw, zc(LANES - KROPE_LO - MLA_ROPE)], axis=1)
    swap = _swap_lanes()
    w_uq = jnp.pad(_bf(p["mla_w_uq"][l]).reshape(MLA_Q_LORA, MLA_HEADS, MLA_DQK), ((0, 0), (0, 0), (0, LANES - MLA_DQK)))
    rot = (np.arange(LANES) >= KROPE_LO) & (np.arange(LANES) < KROPE_LO + MLA_ROPE)
    w_uq_sw = jnp.where(jnp.asarray(rot), w_uq[:, :, swap], jnp.zeros((), BF16))
    w_uq = jnp.concatenate([w_uq.reshape(MLA_Q_LORA, -1), w_uq_sw.reshape(MLA_Q_LORA, -1)], axis=1)
    w_ukv = _bf(p["mla_w_ukv"][l]).reshape(MLA_KV_LORA, MLA_HEADS, MLA_NOPE + MLA_DV)
    w_kn = jnp.pad(w_ukv[:, :, :MLA_NOPE], ((0, 0), (0, 0), (0, LANES - MLA_NOPE)))
    w_vp = jnp.pad(w_ukv[:, :, MLA_NOPE:], ((0, 0), (0, 0), (0, LANES - MLA_DV)))
    w_kv = jnp.concatenate([w_kn.reshape(MLA_KV_LORA, -1), w_vp.reshape(MLA_KV_LORA, -1)], axis=1)

    def pad_gain(g):
        gp = jnp.pad(g, (0, LANES - MLA_DQK))
        return jnp.stack([gp, gp[swap]])

    pad16 = lambda v: jnp.pad(v, (0, LANES - DT_LANES)).reshape(1, LANES)
    w_out = _bf(p["w_out"][l])
    d_ret, d_mla = RET_HEADS * RET_DV, MLA_HEADS * MLA_DV
    hk = RET_HEADS * RET_DK

    def group_halves(wq):
        w5 = wq.reshape(D_MODEL, RET_GROUPS, RET_GH, 2, RET_DK // 2)
        return w5.transpose(0, 1, 3, 2, 4).reshape(D_MODEL, hk)

    w_ret = jnp.concatenate([group_halves(w_in[:, :hk]), group_halves(w_in[:, hk:2 * hk]), w_in[:, 2 * hk:RET_COLS]], axis=1)
    out = {
        "mix_norm": p["mix_norm"][l].reshape(1, -1), "w_ret": w_ret, "w_cq": w_cq,
        "w_ckv": w_ckv, "w_z": w_z, "w_xbc": w_xbc, "w_small": w_small,
        "q_norm": p["mla_q_norm"][l].reshape(1, -1), "kv_norm": p["mla_kv_norm"][l].reshape(1, -1),
        "w_uq": w_uq, "q_gain": pad_gain(p["mla_q_gain"][l]),
        "w_kv": w_kv, "k_gain": pad_gain(p["mla_k_gain"][l]),
        "ret_norm": p["ret_norm"][l].reshape(1, -1),
        "conv_w": p["ssm_conv_w"][l], "conv_b": p["ssm_conv_b"][l].reshape(1, -1),
        "dt_bias": pad16(p["ssm_dt_bias"][l]), "a_log": pad16(p["ssm_a_log"][l]),
        "d_skip": jnp.repeat(p["ssm_d"][l], SSM_HEADDIM).reshape(1, -1), "ssm_norm": p["ssm_norm"][l].reshape(1, -1),
        "wo_ret": w_out[:d_ret], "wo_mla": w_out[d_ret:d_ret + d_mla], "wo_ssm": w_out[d_ret + d_mla:],
    }
    for tag in ("ffn1", "ffn2"):
        out[tag + "_norm"] = p[tag + "_norm"][l].reshape(1, -1)
        out[tag + "_wg"] = _bf(p[tag + "_wgu"][l][:, :D_FF])
        out[tag + "_wu"] = _bf(p[tag + "_wgu"][l][:, D_FF:])
        out[tag + "_wd"] = _bf(p[tag + "_wd"][l])
    return out


def _layer(x, tab, w, *, ret_s0, ssm_h0, conv_buf, past, cfg):
    b, t, _ = x.shape
    n = b * t
    x1 = _ffn_call(x.reshape(n, D_MODEL), [], w["ffn1_norm"], w["ffn1_wg"], w["ffn1_wu"], w["ffn1_wd"], tm=cfg["tm_ffn"])
    if conv_buf is None:
        conv0 = jnp.zeros((b, CONV_PAD, CONV_DIM), F32)
    else:
        conv0 = jnp.concatenate([jnp.zeros((b, CONV_PAD - (CONV_W - 1), CONV_DIM), F32), conv_buf], axis=1)
    rq, rk, rv, rg, mq, ckv, small, z, xbc, kk, vv = _inproj_call(x1, tab, w, tm=cfg["tm_in"], t_per_batch=t)
    r3 = lambda a: a.reshape(b, t, a.shape[-1])
    o_ret, s_ret = _ret_call(r3(rq), r3(rk), r3(rv), r3(rg), w["ret_norm"], ret_s0, c=cfg["c_ret"], nsub=cfg["n_ret"])
    h0 = None if ssm_h0 is None else ssm_h0.reshape(b, SSM_GROUPS, -1, SSM_STATE)
    o_ssm, h_ssm = _ssd_call(r3(z), r3(xbc), r3(small), w, conv0, h0, c=cfg["c_ssd"], nsub=cfg["n_ssd"])
    if past is None:
        o_mla = _mla_prompt_call(r3(mq), r3(kk), r3(vv), tq=cfg["tq"])
    else:
        k_past, v_past, q_pos, past_pos = past
        o_mla = _mla_sample_call(r3(mq), k_past, v_past, r3(kk), r3(vv), q_pos, past_pos)
    f2 = lambda a: a.reshape(n, a.shape[-1])
    x3 = _ffn_call(x1, [(f2(o_ret), w["wo_ret"]), (f2(o_mla), w["wo_mla"]), (f2(o_ssm), w["wo_ssm"])],
                   w["ffn2_norm"], w["ffn2_wg"], w["ffn2_wu"], w["ffn2_wd"], tm=cfg["tm_ffn"])
    keep = CONV_W - 1
    if t >= keep:
        conv_new = r3(xbc)[:, t - keep:]
    else:
        conv_new = jnp.concatenate([conv0[:, CONV_PAD - keep:], r3(xbc)], axis=1)[:, t:]
    state = (r3(ckv), r3(small)[:, :, KROPE_LO:KROPE_LO + MLA_ROPE], s_ret,
             h_ssm.reshape(b, SSM_HEADS, SSM_HEADDIM, SSM_STATE), conv_new)
    return x3.reshape(b, t, D_MODEL), state


PROMPT_CFG = dict(tm_ffn=512, tm_in=512, c_ret=128, n_ret=4, c_ssd=128, n_ssd=4, tq=256)
SAMPLE_CFG = dict(tm_ffn=256, tm_in=256, c_ret=128, n_ret=1, c_ssd=128, n_ssd=1, tq=256)


def kernel(x_prompt, x_sample, cache_mla_ckv, cache_mla_krope, state_ret, state_ssm, state_conv, ffn1_norm, ffn1_wgu, ffn1_wd, mix_norm, w_in, ret_norm, mla_q_norm, mla_kv_norm, mla_w_uq, mla_w_ukv, mla_q_gain, mla_k_gain, ssm_conv_w, ssm_conv_b, ssm_dt_bias, ssm_a_log, ssm_d, ssm_norm, w_out, ffn2_norm, ffn2_wgu, ffn2_wd):
    p = dict(ffn1_norm=ffn1_norm, ffn1_wgu=ffn1_wgu, ffn1_wd=ffn1_wd, mix_norm=mix_norm, w_in=w_in, ret_norm=ret_norm,
             mla_q_norm=mla_q_norm, mla_kv_norm=mla_kv_norm, mla_w_uq=mla_w_uq, mla_w_ukv=mla_w_ukv,
             mla_q_gain=mla_q_gain, mla_k_gain=mla_k_gain, ssm_conv_w=ssm_conv_w, ssm_conv_b=ssm_conv_b,
             ssm_dt_bias=ssm_dt_bias, ssm_a_log=ssm_a_log, ssm_d=ssm_d, ssm_norm=ssm_norm, w_out=w_out,
             ffn2_norm=ffn2_norm, ffn2_wgu=ffn2_wgu, ffn2_wd=ffn2_wd)
    depth = ffn1_norm.shape[0]
    b_s, t_s, _ = x_sample.shape
    t_p = x_prompt.shape[1]
    past = cache_mla_ckv.shape[2]
    q_pos_s = past + np.arange(t_s)
    past_pos = np.arange(past)
    tab_p = _rope_tables(jnp.arange(t_p, dtype=jnp.int32))
    tab_s = jnp.tile(_rope_tables(jnp.asarray(q_pos_s, jnp.int32)), (b_s, 1))
    tab_past = _rope_tables(jnp.arange(past, dtype=jnp.int32))[:, 2 * LANES:]
    yp, ys = x_prompt, x_sample
    new_p, new_s = [[] for _ in range(5)], [[] for _ in range(5)]
    for l in range(depth):
        w = _layer_weights(l, p)
        yp, st_p = _layer(yp, tab_p, w, ret_s0=None, ssm_h0=None, conv_buf=None, past=None, cfg=PROMPT_CFG)
        k_past, v_past = _kvprep_call(cache_mla_ckv, cache_mla_krope, l, w["w_kv"], w["k_gain"], tab_past, tm=512)
        ys, st_s = _layer(ys, tab_s, w, ret_s0=state_ret[l], ssm_h0=state_ssm[l], conv_buf=state_conv[l],
                          past=(k_past, v_past, q_pos_s, past_pos), cfg=SAMPLE_CFG)
        for i in range(5):
            new_p[i].append(st_p[i])
            new_s[i].append(st_s[i])
    return (yp, ys) + tuple(jnp.stack(a) for a in new_p) + tuple(jnp.stack(a) for a in new_s)
```

```python
import functools

import numpy as np
import jax
import jax.numpy as jnp
from jax import lax
from jax.experimental import pallas as pl
from jax.experimental.pallas import tpu as pltpu

D_MODEL = 1024
D_FF = 2816
CHUNK = 64
EPS = 1e-6
RET_HEADS = 8
RET_DK = 64
RET_DV = 64
MLA_HEADS = 8
MLA_Q_LORA = 384
MLA_KV_LORA = 256
MLA_NOPE = 64
MLA_ROPE = 32
MLA_DQK = MLA_NOPE + MLA_ROPE
MLA_DV = 64
ROPE_BASE = 10000.0
SSM_HEADS = 16
SSM_HEADDIM = 64
SSM_D_INNER = SSM_HEADS * SSM_HEADDIM
SSM_GROUPS = 2
SSM_STATE = 128
CONV_W = 4
CONV_DIM = SSM_D_INNER + 2 * SSM_GROUPS * SSM_STATE
RET_COLS = 2 * RET_HEADS * RET_DK + 2 * RET_HEADS * RET_DV
MLA_COLS = MLA_Q_LORA + MLA_KV_LORA + MLA_ROPE

LANES = 128
HALF = LANES // 2
N_PAIRS = 4
RET_GROUPS = 2
RET_GH = RET_HEADS // RET_GROUPS
RET_GW = RET_GH * RET_DK
LOG2E = 1.4426950408889634
CONV_PAD = 8
DT_LANES = SSM_HEADS
KROPE_LO = MLA_NOPE
V7X_VMEM_LIMIT_BYTES = 56 * 1024 * 1024

F32 = jnp.float32
BF16 = jnp.bfloat16


def _bf(x):
    return x.astype(BF16)


def _dot(a, b):
    return jnp.dot(a, b, preferred_element_type=F32)


def _dot_nt(a, b):
    return lax.dot_general(a, b, (((1,), (1,)), ((), ())), preferred_element_type=F32)


def _sigmoid(x):
    return 1.0 / (1.0 + jnp.exp(-x))


def _silu(x):
    return x * _sigmoid(x)


def _rms(x):
    return x * lax.rsqrt(jnp.mean(x * x, axis=-1, keepdims=True) + EPS)


def _split3(x):
    hi = _bf(x)
    r = x - hi.astype(F32)
    mid = _bf(r)
    lo = _bf(r - mid.astype(F32))
    return hi, mid, lo


def _exact01(m01, x):
    hi, mid, lo = _split3(x)
    return _dot(m01, hi) + _dot(m01, mid) + _dot(m01, lo)


def _exact01_nt(m01, x):
    hi, mid, lo = _split3(x)
    return _dot_nt(m01, hi) + _dot_nt(m01, mid) + _dot_nt(m01, lo)


def _exact01_tn(x, m01):
    hi, mid, lo = _split3(x)
    return _dot(hi, m01) + _dot(mid, m01) + _dot(lo, m01)


def _dot_hi_mid(x, m01):
    hi = _bf(x)
    return _dot(hi, m01) + _dot(_bf(x - hi.astype(F32)), m01)


def _lane_lo(rows):
    return lax.broadcasted_iota(jnp.int32, (rows, LANES), 1) < HALF


def _conv_silu_slab(xbuf, j, x_new, cw, cb):
    rows = x_new.shape[0]
    xbuf[j, CONV_PAD:CONV_PAD + rows, :] = x_new
    conv = cb
    for wi in range(CONV_W):
        off = CONV_PAD - (CONV_W - 1) + wi
        conv = conv + cw[wi:wi + 1, :] * xbuf[j, off:off + rows, :]
    xbuf[j, 0:CONV_PAD, :] = xbuf[j, rows:rows + CONV_PAD, :]
    return _silu(conv)


def _const_spec(shape):
    nd = len(shape)
    return pl.BlockSpec(shape, lambda *_: (0,) * nd, pipeline_mode=pl.Buffered(1))


def _params(n_axes):
    return pltpu.CompilerParams(dimension_semantics=("arbitrary",) * n_axes,
                                vmem_limit_bytes=V7X_VMEM_LIMIT_BYTES)


def _ffn_body(*refs, n_pre, fc):
    x_ref = refs[0]
    pre = [(refs[1 + 2 * i], refs[2 + 2 * i]) for i in range(n_pre)]
    g_ref, wg_ref, wu_ref, wd_ref, o_ref = refs[1 + 2 * n_pre:]
    x = x_ref[...]
    for m_ref, w_ref in pre:
        x = x + _dot(m_ref[...], w_ref[...])
    xn = _bf(_rms(x) * g_ref[...])
    acc = None
    for c in range(D_FF // fc):
        gate = _dot(xn, wg_ref[:, c * fc:(c + 1) * fc])
        up = _dot(xn, wu_ref[:, c * fc:(c + 1) * fc])
        part = _dot(_bf(_silu(gate) * up), wd_ref[c * fc:(c + 1) * fc, :])
        acc = part if acc is None else acc + part
    o_ref[...] = x + 0.5 * acc


def _resident_block(block, index):
    return pl.BlockSpec(block, lambda *_: index, pipeline_mode=pl.Buffered(1))


def _ffn_call(x, pre, w_out, gain, wgu, wd, layer, *, tm, fc=256):
    n = x.shape[0]
    tm = min(tm, n)
    assert n % tm == 0 and D_FF % fc == 0
    args, specs = [x], [pl.BlockSpec((tm, D_MODEL), lambda i: (i, 0))]
    row = 0
    for m in pre:
        width = m.shape[1]
        assert row % width == 0
        args += [m, w_out]
        specs += [pl.BlockSpec((tm, width), lambda i: (i, 0)),
                  _resident_block((None, width, D_MODEL), (layer, row // width, 0))]
        row += width
    args += [gain, wgu, wgu, wd]
    specs += [_const_spec(gain.shape), _resident_block((None, D_MODEL, D_FF), (layer, 0, 0)),
              _resident_block((None, D_MODEL, D_FF), (layer, 0, 1)), _resident_block((None, D_FF, D_MODEL), (layer, 0, 0))]
    return pl.pallas_call(
        functools.partial(_ffn_body, n_pre=len(pre), fc=fc),
        grid=(n // tm,),
        in_specs=specs,
        out_specs=pl.BlockSpec((tm, D_MODEL), lambda i: (i, 0)),
        out_shape=jax.ShapeDtypeStruct((n, D_MODEL), F32),
        compiler_params=_params(1),
        name="ffn",
    )(*args)


def _kv_from_latent(ckv_n, krope_slab, krope_sw_slab, wkv_ref, kgain, kgain_sw, cos, sin_s, k_ref, v_ref):
    kvp = _dot(_bf(ckv_n), wkv_ref[...])
    ones_hi = jnp.where(_lane_lo(kvp.shape[0]), 0.0, 1.0)
    g_cos = kgain * cos
    rot = krope_sw_slab * (kgain_sw * sin_s)
    for h in range(MLA_HEADS):
        kh = kvp[:, h * LANES:(h + 1) * LANES] + krope_slab
        ss = jnp.sum(kh * kh, axis=-1, keepdims=True) * (1.0 / MLA_DQK)
        k_ref[:, h * LANES:(h + 1) * LANES] = _bf((kh * g_cos + rot) * lax.rsqrt(ss + EPS))
        vs = slice((MLA_HEADS + h) * LANES, (MLA_HEADS + h + 1) * LANES)
        v_ref[:, h * LANES:(h + 1) * LANES] = _bf(kvp[:, vs] + ones_hi)


def _kvprep_body(ckv_ref, kr_ref, place_ref, wkv_ref, kgain_ref, tab_ref, k_ref, v_ref):
    slabs = _exact01_tn(kr_ref[...], place_ref[...])
    tab = tab_ref[...]
    kgain = kgain_ref[...]
    _kv_from_latent(ckv_ref[...], slabs[:, :LANES], slabs[:, LANES:], wkv_ref, kgain[0:1], kgain[1:2],
                    tab[:, :LANES], tab[:, LANES:], k_ref, v_ref)


def _swap_lanes():
    idx = np.arange(LANES)
    half = MLA_ROPE // 2
    idx[KROPE_LO:KROPE_LO + half] += half
    idx[KROPE_LO + half:KROPE_LO + MLA_ROPE] -= half
    return idx


def _kvprep_call(ckv, krope, layer, wkv, kgain, tab, *, tm):
    _, b, s, _ = ckv.shape
    tm = min(tm, s)
    assert s % tm == 0
    place = np.zeros((MLA_ROPE, 2 * LANES), np.float32)
    place[np.arange(MLA_ROPE), KROPE_LO + np.arange(MLA_ROPE)] = 1.0
    place[np.arange(MLA_ROPE), LANES + _swap_lanes()[KROPE_LO:KROPE_LO + MLA_ROPE]] = 1.0
    place = jnp.asarray(place, BF16)
    return pl.pallas_call(
        _kvprep_body,
        grid=(b, s // tm),
        in_specs=[pl.BlockSpec((None, None, tm, MLA_KV_LORA), lambda i, j: (layer, i, j, 0)),
                  pl.BlockSpec((None, None, tm, MLA_ROPE), lambda i, j: (layer, i, j, 0)),
                  _const_spec(place.shape), _const_spec(wkv.shape), _const_spec(kgain.shape),
                  pl.BlockSpec((tm, 2 * LANES), lambda i, j: (j, 0))],
        out_specs=[pl.BlockSpec((None, tm, MLA_HEADS * LANES), lambda i, j: (i, j, 0))] * 2,
        out_shape=[jax.ShapeDtypeStruct((b, s, MLA_HEADS * LANES), BF16)] * 2,
        compiler_params=_params(2),
        name="kvprep",
    )(ckv, krope, place, wkv, kgain, tab)


def _inproj_body(x_ref, tab_ref, gmix_ref, wret_ref, wcq_ref, wckv_ref, wz_ref, wxbc_ref, wsm_ref,
                 gq_ref, gkv_ref, wuq_ref, qgain_ref, wkv_ref, kgain_ref,
                 rq_ref, rk_ref, rv_ref, rg_ref, mq_ref, ckv_ref, sm_ref, z_ref, xbc_ref, k_ref, v_ref):
    hn = _bf(_rms(x_ref[...]) * gmix_ref[...])
    tab = tab_ref[...]
    cos_r, sin_r = tab[:, 0:LANES], tab[:, LANES:2 * LANES]
    cos_m, sin_s = tab[:, 2 * LANES:3 * LANES], tab[:, 3 * LANES:4 * LANES]

    xbc_ref[...] = _dot(hn, wxbc_ref[...])
    z_ref[...] = _dot(hn, wz_ref[...])

    hk = RET_HEADS * RET_DK
    ret = _dot(hn, wret_ref[...])
    for out_ref, base, scale in ((rq_ref, 0, 1.0), (rk_ref, hk, RET_DK ** -0.5)):
        for g in range(RET_GROUPS):
            lo_sl = slice(g * RET_GW, g * RET_GW + LANES)
            hi_sl = slice(g * RET_GW + LANES, (g + 1) * RET_GW)
            x1 = ret[:, base + g * RET_GW:base + g * RET_GW + LANES]
            x2 = ret[:, base + g * RET_GW + LANES:base + (g + 1) * RET_GW]
            out_ref[:, lo_sl] = _bf((x1 * cos_r - x2 * sin_r) * scale)
            out_ref[:, hi_sl] = _bf((x1 * sin_r + x2 * cos_r) * scale)
    rv_ref[...] = _bf(ret[:, 2 * hk:3 * hk])
    rg_ref[...] = _silu(ret[:, 3 * hk:])

    cq = _bf(_rms(_dot(hn, wcq_ref[...])) * gq_ref[...])
    qp = _dot(cq, wuq_ref[...])
    qgain = qgain_ref[...]
    q_scale = (MLA_DQK ** -0.5) * LOG2E
    g_cos = qgain[0:1] * cos_m * q_scale
    g_sin = qgain[1:2] * sin_s * q_scale
    for h in range(MLA_HEADS):
        qh = qp[:, h * LANES:(h + 1) * LANES]
        qs = qp[:, (MLA_HEADS + h) * LANES:(MLA_HEADS + h + 1) * LANES]
        ss = jnp.sum(qh * qh, axis=-1, keepdims=True) * (1.0 / MLA_DQK)
        mq_ref[:, h * LANES:(h + 1) * LANES] = _bf((qh * g_cos + qs * g_sin) * lax.rsqrt(ss + EPS))

    small2 = _dot(hn, wsm_ref[...])
    small = small2[:, :LANES]
    sm_ref[...] = small
    ckv_n = _rms(_dot(hn, wckv_ref[...])) * gkv_ref[...]
    ckv_ref[...] = ckv_n
    lane = lax.broadcasted_iota(jnp.int32, small.shape, 1)
    rot_lanes = (lane >= KROPE_LO) & (lane < KROPE_LO + MLA_ROPE)
    kgain = kgain_ref[...]
    _kv_from_latent(ckv_n, jnp.where(rot_lanes, small, 0.0), jnp.where(rot_lanes, small2[:, LANES:], 0.0),
                    wkv_ref, kgain[0:1], kgain[1:2], cos_m, sin_s, k_ref, v_ref)


def _inproj_call(x, tab, w, *, tm, t_per_batch):
    n = x.shape[0]
    tm = min(tm, n)
    assert n % tm == 0
    if tm <= t_per_batch:
        assert t_per_batch % tm == 0 and tab.shape[0] == t_per_batch
        per = t_per_batch // tm
        tab_spec = pl.BlockSpec((tm, 4 * LANES), lambda i: (i % per, 0))
    else:
        assert tab.shape[0] == tm
        tab_spec = pl.BlockSpec((tm, 4 * LANES), lambda i: (0, 0))
    consts = [w["mix_norm"], w["w_ret"], w["w_cq"], w["w_ckv"], w["w_z"], w["w_xbc"], w["w_small"],
              w["q_norm"], w["kv_norm"], w["w_uq"], w["q_gain"], w["w_kv"], w["k_gain"]]
    widths = [(512, BF16), (512, BF16), (512, BF16), (512, F32), (MLA_HEADS * LANES, BF16),
              (MLA_KV_LORA, F32), (LANES, F32), (SSM_D_INNER, F32), (CONV_DIM, F32),
              (MLA_HEADS * LANES, BF16), (MLA_HEADS * LANES, BF16)]
    return pl.pallas_call(
        _inproj_body,
        grid=(n // tm,),
        in_specs=[pl.BlockSpec((tm, D_MODEL), lambda i: (i, 0)), tab_spec] + [_const_spec(c.shape) for c in consts],
        out_specs=[pl.BlockSpec((tm, wd), lambda i: (i, 0)) for wd, _ in widths],
        out_shape=[jax.ShapeDtypeStruct((n, wd), dt) for wd, dt in widths],
        compiler_params=_params(1),
        name="inproj",
    )(x, tab, *consts)


def _ret_body(*refs, has_s0, c, nsub):
    (q_ref, k_ref, v_ref, g_ref, dm_ref, qd_ref, kd_ref, cd_ref, bd_ref, eye_ref, gn_ref, hm_ref, vm_ref) = refs[:13]
    s0_ref = refs[13] if has_s0 else None
    o_ref, sout_ref, s_ref = refs[-3:]
    half_dk = RET_DK // 2

    def state_blocks():
        for g in range(RET_GROUPS):
            for hh in range(RET_GH):
                cols = slice(hh * RET_DV, (hh + 1) * RET_DV)
                for part in range(2):
                    r0 = part * LANES + hh * half_dk
                    yield g * RET_GH + hh, slice(part * half_dk, (part + 1) * half_dk), g, slice(r0, r0 + half_dk), cols

    @pl.when(pl.program_id(1) == 0)
    def _():
        s_ref[...] = jnp.zeros(s_ref.shape, F32)
        if has_s0:
            for h, hrows, g, rows_, cols in state_blocks():
                s_ref[g, rows_, cols] = s0_ref[h, hrows, :]

    lo = _lane_lo(c)
    eye = eye_ref[...]
    for sub in range(nsub):
        rs = slice(sub * c, (sub + 1) * c)
        for g in range(RET_GROUPS):
            gs = slice(g * RET_GW, (g + 1) * RET_GW)
            qg, kg, vg = q_ref[rs, gs], k_ref[rs, gs], v_ref[rs, gs]
            state = s_ref[g]
            o = _dot(qg, _bf(state)) * qd_ref[:, gs]
            for hh in range(RET_GH):
                att = _dot_nt(qg * hm_ref[hh], kg) * dm_ref[g * RET_GH + hh]
                o = o + _dot(_bf(att), vg * vm_ref[hh])
            for half in range(2):
                sl = slice(g * RET_GW + half * LANES, g * RET_GW + (half + 1) * LANES)
                oh = o[:, half * LANES:(half + 1) * LANES]
                oo = oh * oh
                s_lo = jnp.sum(jnp.where(lo, oo, 0.0), axis=-1, keepdims=True)
                s_hi = jnp.sum(jnp.where(lo, 0.0, oo), axis=-1, keepdims=True)
                ss = jnp.where(lo, s_lo, s_hi) * (1.0 / RET_DV)
                o_ref[rs, sl] = _bf(oh * lax.rsqrt(ss + EPS) * gn_ref[:, sl] * g_ref[rs, sl])
            kdec = _bf(kg.astype(F32) * kd_ref[:, gs])
            kdec_t = _bf(_dot_nt(eye, kdec))
            s_ref[g] = state * cd_ref[g] + _dot(kdec_t, vg) * bd_ref[...]

    @pl.when(pl.program_id(1) == pl.num_programs(1) - 1)
    def _():
        for h, hrows, g, rows_, cols in state_blocks():
            sout_ref[h, hrows, :] = s_ref[g, rows_, cols]


def _ret_qk_head(width):
    return (np.arange(width) % LANES) // (RET_DK // 2) + RET_GH * (np.arange(width) // RET_GW)


def _ret_tables(c):
    log_g = jnp.log(1.0 - 2.0 ** (-5.0 - jnp.arange(RET_HEADS, dtype=F32)))
    idx = jnp.arange(c, dtype=F32)
    diff = idx[:, None] - idx[None, :]
    dmask = jnp.where(diff >= 0, jnp.exp(log_g[:, None, None] * jnp.maximum(diff, 0.0)), 0.0)
    q_dec = jnp.exp(log_g[:, None] * (idx[None, :] + 1.0))
    k_dec = jnp.exp(log_g[:, None] * (c - 1.0 - idx[None, :]))
    c_dec = jnp.exp(log_g * c)
    w = RET_HEADS * RET_DK
    qk_head = _ret_qk_head(w)
    v_head = np.arange(w) // RET_DV
    qd = q_dec.T[:, v_head]
    kd = k_dec.T[:, qk_head]
    cd = jnp.broadcast_to(c_dec[qk_head].reshape(RET_GROUPS, RET_GW, 1), (RET_GROUPS, RET_GW, RET_GW))
    bd = jnp.asarray((qk_head[:RET_GW, None] == v_head[None, :RET_GW]).astype(np.float32))
    hm = np.stack([np.broadcast_to(qk_head[:RET_GW] == hh, (c, RET_GW)) for hh in range(RET_GH)])
    vm = np.stack([np.broadcast_to(v_head[:RET_GW] == hh, (c, RET_GW)) for hh in range(RET_GH)])
    return dmask, qd, kd, cd, bd, jnp.asarray(hm, BF16), jnp.asarray(vm, BF16)


def _ret_call(rq, rk, rv, rg, gn, s0, layer, *, c, nsub):
    b, t, w = rq.shape
    c = min(c, t)
    nsub = min(nsub, t // c)
    blk = c * nsub
    assert t % blk == 0
    dmask, qd, kd, cd, bd, hm, vm = _ret_tables(c)
    eye = jnp.eye(RET_GW, dtype=BF16)
    tok = lambda i, j: (i, j, 0)
    args = [rq, rk, rv, rg, dmask, qd, kd, cd, bd, eye, gn, hm, vm]
    specs = [pl.BlockSpec((None, blk, w), tok)] * 4 + [_const_spec(a.shape) for a in args[4:]]
    st_spec = pl.BlockSpec((None, RET_HEADS, RET_DK, RET_DV), lambda i, j: (i, 0, 0, 0))
    if s0 is not None:
        args.append(s0)
        specs.append(pl.BlockSpec((None, None, RET_HEADS, RET_DK, RET_DV), lambda i, j: (layer, i, 0, 0, 0)))
    return pl.pallas_call(
        functools.partial(_ret_body, has_s0=s0 is not None, c=c, nsub=nsub),
        grid=(b, t // blk),
        in_specs=specs,
        out_specs=[pl.BlockSpec((None, blk, w), tok), st_spec],
        out_shape=[jax.ShapeDtypeStruct((b, t, w), BF16),
                   jax.ShapeDtypeStruct((b, RET_HEADS, RET_DK, RET_DV), F32)],
        scratch_shapes=[pltpu.VMEM((RET_GROUPS, RET_GW, RET_GW), F32)],
        compiler_params=_params(2),
        name="retention",
    )(*args)


def _ssd_body(*refs, has_h0, c, nsub):
    (z_ref, xbc_ref, sm_ref, cw_ref, cb_ref, dtb_ref, alog_ref, dsk_ref, ng_ref,
     tri_ref, eye_ref, spread_ref, conv0_ref) = refs[:13]
    h0_ref = refs[13] if has_h0 else None
    o_ref, h_ref, xbuf, st = refs[-4:]
    rows = z_ref.shape[0]
    ci = pl.program_id(1)

    n_slabs = CONV_DIM // LANES

    @pl.when(ci == 0)
    def _():
        for j in range(n_slabs):
            xbuf[j, 0:CONV_PAD, :] = conv0_ref[:, j * LANES:(j + 1) * LANES]
        for g in range(SSM_GROUPS):
            st[g] = h0_ref[g].T if has_h0 else jnp.zeros(st.shape[1:], F32)

    act_slabs = []
    for j in range(n_slabs):
        sl = slice(j * LANES, (j + 1) * LANES)
        act_slabs.append(_conv_silu_slab(xbuf, j, xbc_ref[:, sl], cw_ref[:, sl], cb_ref[:, sl]))
    x_slab0, b_slab0, c_slab0 = 0, SSM_D_INNER // LANES, SSM_D_INNER // LANES + SSM_GROUPS

    lane = lax.broadcasted_iota(jnp.int32, (rows, LANES), 1)
    lane1 = lax.broadcasted_iota(jnp.int32, (1, LANES), 1)
    dt_raw = jnp.where(lane < DT_LANES, sm_ref[...], 0.0) + dtb_ref[...]
    dt_all = jnp.maximum(dt_raw, 0.0) + jnp.log1p(jnp.exp(-jnp.abs(dt_raw)))
    a_neg = jnp.where(lane1 < DT_LANES, -jnp.exp(alog_ref[...]), 0.0)
    dta_all = dt_all * a_neg
    ldt_all = jnp.maximum(jnp.log(dt_all), -1e30)
    eye = eye_ref[...]
    lo = _lane_lo(c)
    lo1 = lane1 < HALF
    m_lo = _bf(jnp.where(lo, 1.0, 0.0))
    m_hi = _bf(jnp.where(lo, 0.0, 1.0))
    r_i = lax.broadcasted_iota(jnp.int32, (c, c), 0)
    c_i = lax.broadcasted_iota(jnp.int32, (c, c), 1)
    causal = r_i >= c_i

    def pair_lanes(v, h):
        return jnp.where(lo1, v[:, h:h + 1], v[:, h + 1:h + 2])

    for sub in range(nsub):
        r0 = sub * c
        rs_ = slice(r0, r0 + c)
        dtv = dt_all[r0:r0 + c]
        a_cum = _exact01(tri_ref[...], dta_all[r0:r0 + c])
        a_col = a_cum * LOG2E
        row_t = _exact01_nt(eye, (a_cum - ldt_all[r0:r0 + c]) * LOG2E)
        a_last = a_cum[c - 1:c, :]
        e_last = jnp.exp(a_last)
        exp_a = _dot_hi_mid(jnp.exp(a_cum), spread_ref[...])
        w_dt = _dot_hi_mid(jnp.exp(a_last - a_cum) * dtv, spread_ref[...])
        for g in range(SSM_GROUPS):
            bmb = _bf(act_slabs[b_slab0 + g][r0:r0 + c])
            cmb = _bf(act_slabs[c_slab0 + g][r0:r0 + c])
            scores = _dot_nt(cmb, bmb)
            y_off = _dot(cmb, _bf(st[g]))
            ssq = jnp.zeros((c, 1), F32)
            gated, xw, e_row = [], [], []
            for jj in range(N_PAIRS):
                j = g * N_PAIRS + jj
                sl = slice(j * LANES, (j + 1) * LANES)
                h0 = 2 * j
                xsl = act_slabs[x_slab0 + j][r0:r0 + c]
                xb = _bf(xsl)
                y = y_off[:, jj * LANES:(jj + 1) * LANES] * exp_a[:, sl] + dsk_ref[:, sl] * xsl
                for e, msk in enumerate((m_lo, m_hi)):
                    h = h0 + e
                    seg = jnp.where(causal, a_col[:, h:h + 1] - row_t[h:h + 1, :], -jnp.inf)
                    y = y + _dot(_bf(scores * jnp.exp2(seg)), xb * msk)
                zs = z_ref[rs_, sl]
                gt = y * _silu(zs)
                ssq = ssq + jnp.sum(gt * gt, axis=-1, keepdims=True)
                gated.append(gt)
                xw.append(_bf(xsl * w_dt[:, sl]))
                e_row.append(pair_lanes(e_last, h0))
            rs = lax.rsqrt(ssq * (1.0 / (SSM_D_INNER // SSM_GROUPS)) + EPS)
            for jj in range(N_PAIRS):
                sl = slice((g * N_PAIRS + jj) * LANES, (g * N_PAIRS + jj + 1) * LANES)
                o_ref[rs_, sl] = _bf(gated[jj] * rs * ng_ref[:, sl])
            bm_t = _bf(_dot_nt(eye, bmb))
            upd = _dot(bm_t, jnp.concatenate(xw, axis=1))
            st[g] = st[g] * jnp.concatenate(e_row, axis=1) + upd

    @pl.when(ci == pl.num_programs(1) - 1)
    def _():
        for g in range(SSM_GROUPS):
            h_ref[g] = st[g].T


def _ssd_call(z, xbc, small, w, conv0, h0, layer, *, c, nsub):
    b, t, _ = z.shape
    c = min(c, t)
    nsub = min(nsub, t // c)
    blk = c * nsub
    assert t % blk == 0 and c >= CONV_PAD
    tri = jnp.asarray(np.tril(np.ones((c, c), np.float32)), BF16)
    eye = jnp.eye(LANES, dtype=BF16)
    spread = np.zeros((LANES, SSM_D_INNER), np.float32)
    spread[np.arange(SSM_D_INNER) // SSM_HEADDIM, np.arange(SSM_D_INNER)] = 1.0
    spread = jnp.asarray(spread, BF16)
    gw = SSM_D_INNER // SSM_GROUPS
    tok = lambda i, j: (i, j, 0)
    consts = [w["conv_w"], w["conv_b"], w["dt_bias"], w["a_log"], w["d_skip"], w["ssm_norm"], tri, eye, spread]
    args = [z, xbc, small] + consts + [conv0]
    specs = ([pl.BlockSpec((None, blk, SSM_D_INNER), tok), pl.BlockSpec((None, blk, CONV_DIM), tok),
              pl.BlockSpec((None, blk, LANES), tok)] + [_const_spec(a.shape) for a in consts]
             + [pl.BlockSpec((None, CONV_PAD, CONV_DIM), lambda i, j: (i, 0, 0))])
    if h0 is not None:
        args.append(h0)
        specs.append(pl.BlockSpec((None, None, SSM_GROUPS, gw, SSM_STATE), lambda i, j: (layer, i, 0, 0, 0)))
    return pl.pallas_call(
        functools.partial(_ssd_body, has_h0=h0 is not None, c=c, nsub=nsub),
        grid=(b, t // blk),
        in_specs=specs,
        out_specs=[pl.BlockSpec((None, blk, SSM_D_INNER), tok),
                   pl.BlockSpec((None, SSM_GROUPS, gw, SSM_STATE), lambda i, j: (i, 0, 0, 0))],
        out_shape=[jax.ShapeDtypeStruct((b, t, SSM_D_INNER), BF16),
                   jax.ShapeDtypeStruct((b, SSM_GROUPS, gw, SSM_STATE), F32)],
        scratch_shapes=[pltpu.VMEM((CONV_DIM // LANES, CONV_PAD + blk, LANES), F32),
                        pltpu.VMEM((SSM_GROUPS, SSM_STATE, gw), F32)],
        compiler_params=_params(2),
        name="ssd",
    )(*args)


def _pair_output(acc0, acc1, lo):
    return jnp.where(lo, acc0 / pltpu.roll(acc0, HALF, 1), pltpu.roll(acc1, HALF, 1) / acc1)


def _mla_prompt_body(q_ref, k_ref, v_ref, o_ref, m_sc, acc_sc, *, tq):
    qi = pl.program_id(1)
    m_sc[...] = jnp.full(m_sc.shape, -jnp.inf, F32)
    acc_sc[...] = jnp.zeros(acc_sc.shape, F32)
    r_i = lax.broadcasted_iota(jnp.int32, (tq, tq), 0)
    c_i = lax.broadcasted_iota(jnp.int32, (tq, tq), 1)
    visible = lax.shift_right_logical(c_i, 6) <= lax.shift_right_logical(r_i, 6)

    def step(t, diagonal):
        off = pl.multiple_of(t * tq, tq)
        for h in range(MLA_HEADS):
            hs = slice(h * LANES, (h + 1) * LANES)
            s = _dot_nt(q_ref[:, hs], k_ref[pl.ds(off, tq), hs])
            if diagonal:
                s = jnp.where(visible, s, -jnp.inf)
            m_old = m_sc[h]
            m_new = jnp.maximum(m_old, jnp.max(s, axis=-1, keepdims=True))
            alpha = jnp.exp2(m_old - m_new)
            p = jnp.exp2(s - jnp.concatenate([m_new] * (tq // LANES), axis=1))
            acc_sc[h] = acc_sc[h] * alpha + _dot(_bf(p), v_ref[pl.ds(off, tq), hs])
            m_sc[h] = m_new

    def body(t, carry):
        step(t, False)
        return carry

    lax.fori_loop(0, qi, body, 0)
    step(qi, True)
    lo = _lane_lo(tq)
    for j in range(N_PAIRS):
        o_ref[:, j * LANES:(j + 1) * LANES] = _bf(_pair_output(acc_sc[2 * j], acc_sc[2 * j + 1], lo))


def _mla_prompt_call(q, k, v, *, tq):
    b, t, _ = q.shape
    tq = min(tq, t)
    assert t % tq == 0 and tq % LANES == 0 and CHUNK == 64
    kw = MLA_HEADS * LANES
    return pl.pallas_call(
        functools.partial(_mla_prompt_body, tq=tq),
        grid=(b, t // tq),
        in_specs=[pl.BlockSpec((None, tq, kw), lambda i, j: (i, j, 0)),
                  pl.BlockSpec((None, t, kw), lambda i, j: (i, 0, 0)),
                  pl.BlockSpec((None, t, kw), lambda i, j: (i, 0, 0))],
        out_specs=pl.BlockSpec((None, tq, MLA_HEADS * MLA_DV), lambda i, j: (i, j, 0)),
        out_shape=jax.ShapeDtypeStruct((b, t, MLA_HEADS * MLA_DV), BF16),
        scratch_shapes=[pltpu.VMEM((MLA_HEADS, tq, LANES), F32), pltpu.VMEM((MLA_HEADS, tq, LANES), F32)],
        compiler_params=_params(2),
        name="mla_prompt",
    )(q, k, v)


def _mla_sample_body(q_ref, kp_ref, vp_ref, kn_ref, vn_ref, visp_ref, visn_ref, o_ref):
    lo = _lane_lo(q_ref.shape[0])
    vis_p = visp_ref[...] > 0.5
    vis_n = visn_ref[...] > 0.5
    for j in range(N_PAIRS):
        accs = []
        for h in (2 * j, 2 * j + 1):
            hs = slice(h * LANES, (h + 1) * LANES)
            qh = q_ref[:, hs]
            s_p = jnp.where(vis_p, _dot_nt(qh, kp_ref[:, hs]), -jnp.inf)
            s_n = jnp.where(vis_n, _dot_nt(qh, kn_ref[:, hs]), -jnp.inf)
            m = jnp.maximum(jnp.max(s_p, axis=-1, keepdims=True), jnp.max(s_n, axis=-1, keepdims=True))
            accs.append(_dot(_bf(jnp.exp2(s_p - m)), vp_ref[:, hs]) + _dot(_bf(jnp.exp2(s_n - m)), vn_ref[:, hs]))
        o_ref[:, j * LANES:(j + 1) * LANES] = _bf(_pair_output(accs[0], accs[1], lo))


def _mla_sample_call(q, k_past, v_past, k_new, v_new, q_pos, past_pos):
    b, t, _ = q.shape
    s = k_past.shape[1]
    vis_p = jnp.asarray((past_pos[None, :] // CHUNK <= q_pos[:, None] // CHUNK).astype(np.float32))
    vis_n = jnp.asarray((q_pos[None, :] // CHUNK <= q_pos[:, None] // CHUNK).astype(np.float32))
    kw, vw = MLA_HEADS * LANES, MLA_HEADS * MLA_DV
    bat = lambda i: (i, 0, 0)
    return pl.pallas_call(
        _mla_sample_body,
        grid=(b,),
        in_specs=[pl.BlockSpec((None, t, kw), bat), pl.BlockSpec((None, s, kw), bat),
                  pl.BlockSpec((None, s, kw), bat), pl.BlockSpec((None, t, kw), bat),
                  pl.BlockSpec((None, t, kw), bat), _const_spec(vis_p.shape), _const_spec(vis_n.shape)],
        out_specs=pl.BlockSpec((None, t, vw), bat),
        out_shape=jax.ShapeDtypeStruct((b, t, vw), BF16),
        compiler_params=_params(1),
        name="mla_sample",
    )(q, k_past, v_past, k_new, v_new, vis_p, vis_n)


def _rope_tables(pos):
    posf = np.asarray(pos, np.float32)
    th_r = (1.0 / (10000.0 ** np.linspace(0.0, 1.0, RET_DK // 2, dtype=np.float32))).astype(np.float32)
    ang = posf[:, None] * th_r[None, :]
    c, s = np.cos(ang), np.sin(ang)
    ret = [np.concatenate([c] * RET_GH, 1), np.concatenate([s] * RET_GH, 1)]
    th_m = (1.0 / (ROPE_BASE ** (np.arange(0, MLA_ROPE, 2, dtype=np.float32) / MLA_ROPE))).astype(np.float32)
    angm = posf[:, None] * th_m[None, :]
    cm, sm = np.cos(angm), np.sin(angm)
    n = posf.shape[0]
    one, z64 = np.ones((n, MLA_NOPE), np.float32), np.zeros((n, MLA_NOPE), np.float32)
    z32 = np.zeros((n, LANES - MLA_DQK), np.float32)
    mla = [np.concatenate([one, cm, cm, z32], 1), np.concatenate([z64, -sm, sm, z32], 1)]
    return np.concatenate(ret + mla, axis=1).astype(np.float32)


def _layer_weights(l, p):
    w_in = _bf(p["w_in"][l])
    o = RET_COLS
    w_cq, w_ckv = w_in[:, o:o + MLA_Q_LORA], w_in[:, o + MLA_Q_LORA:o + MLA_Q_LORA + MLA_KV_LORA]
    w_kr = w_in[:, o + MLA_Q_LORA + MLA_KV_LORA:o + MLA_COLS]
    o += MLA_COLS
    w_z, w_xbc, w_dt = w_in[:, o:o + SSM_D_INNER], w_in[:, o + SSM_D_INNER:o + SSM_D_INNER + CONV_DIM], w_in[:, o + SSM_D_INNER + CONV_DIM:]
    zc = lambda n: jnp.zeros((D_MODEL, n), BF16)
    half = MLA_ROPE // 2
    w_kr_sw = jnp.concatenate([w_kr[:, half:], w_kr[:, :half]], axis=1)
    small_cols = lambda kr: [w_dt, zc(KROPE_LO - DT_LANES), kr, zc(LANES - KROPE_LO - MLA_ROPE)]
    w_small = jnp.concatenate(small_cols(w_kr) + [zc(KROPE_LO), w_kr_sw, zc(LANES - KROPE_LO - MLA_ROPE)], axis=1)
    swap = _swap_lanes()
    w_uq = jnp.pad(_bf(p["mla_w_uq"][l]).reshape(MLA_Q_LORA, MLA_HEADS, MLA_DQK), ((0, 0), (0, 0), (0, LANES - MLA_DQK)))
    rot = (np.arange(LANES) >= KROPE_LO) & (np.arange(LANES) < KROPE_LO + MLA_ROPE)
    w_uq_sw = jnp.where(jnp.asarray(rot), w_uq[:, :, swap], jnp.zeros((), BF16))
    w_uq = jnp.concatenate([w_uq.reshape(MLA_Q_LORA, -1), w_uq_sw.reshape(MLA_Q_LORA, -1)], axis=1)
    w_ukv = _bf(p["mla_w_ukv"][l]).reshape(MLA_KV_LORA, MLA_HEADS, MLA_NOPE + MLA_DV)
    w_kn = jnp.pad(w_ukv[:, :, :MLA_NOPE], ((0, 0), (0, 0), (0, LANES - MLA_NOPE)))
    w_vp = jnp.pad(w_ukv[:, :, MLA_NOPE:], ((0, 0), (0, 0), (0, LANES - MLA_DV)))
    w_kv = jnp.concatenate([w_kn.reshape(MLA_KV_LORA, -1), w_vp.reshape(MLA_KV_LORA, -1)], axis=1)

    def pad_gain(g):
        gp = jnp.pad(g, (0, LANES - MLA_DQK))
        return jnp.stack([gp, gp[swap]])

    pad16 = lambda v: jnp.pad(v, (0, LANES - DT_LANES)).reshape(1, LANES)
    hk = RET_HEADS * RET_DK

    def group_halves(wq):
        w5 = wq.reshape(D_MODEL, RET_GROUPS, RET_GH, 2, RET_DK // 2)
        return w5.transpose(0, 1, 3, 2, 4).reshape(D_MODEL, hk)

    w_ret = jnp.concatenate([group_halves(w_in[:, :hk]), group_halves(w_in[:, hk:2 * hk]), w_in[:, 2 * hk:RET_COLS]], axis=1)
    out = {
        "mix_norm": p["mix_norm"][l].reshape(1, -1), "w_ret": w_ret, "w_cq": w_cq,
        "w_ckv": w_ckv, "w_z": w_z, "w_xbc": w_xbc, "w_small": w_small,
        "q_norm": p["mla_q_norm"][l].reshape(1, -1), "kv_norm": p["mla_kv_norm"][l].reshape(1, -1),
        "w_uq": w_uq, "q_gain": pad_gain(p["mla_q_gain"][l]),
        "w_kv": w_kv, "k_gain": pad_gain(p["mla_k_gain"][l]),
        "ret_norm": p["ret_norm"][l].reshape(1, -1),
        "conv_w": p["ssm_conv_w"][l], "conv_b": p["ssm_conv_b"][l].reshape(1, -1),
        "dt_bias": pad16(p["ssm_dt_bias"][l]), "a_log": pad16(p["ssm_a_log"][l]),
        "d_skip": jnp.repeat(p["ssm_d"][l], SSM_HEADDIM).reshape(1, -1), "ssm_norm": p["ssm_norm"][l].reshape(1, -1),
        "ffn1_norm": p["ffn1_norm"][l].reshape(1, -1), "ffn2_norm": p["ffn2_norm"][l].reshape(1, -1),
    }
    return out


def _layer(x, tab, w, wb, layer, *, ret_s0, ssm_h0, conv_buf, past, cfg):
    b, t, _ = x.shape
    n = b * t
    x1 = _ffn_call(x.reshape(n, D_MODEL), [], None, w["ffn1_norm"], wb["ffn1_wgu"], wb["ffn1_wd"], layer, tm=cfg["tm_ffn"])
    if conv_buf is None:
        conv0 = jnp.zeros((b, CONV_PAD, CONV_DIM), F32)
    else:
        conv0 = jnp.concatenate([jnp.zeros((b, CONV_PAD - (CONV_W - 1), CONV_DIM), F32), conv_buf], axis=1)
    rq, rk, rv, rg, mq, ckv, small, z, xbc, kk, vv = _inproj_call(x1, tab, w, tm=cfg["tm_in"], t_per_batch=t)
    r3 = lambda a: a.reshape(b, t, a.shape[-1])
    o_ret, s_ret = _ret_call(r3(rq), r3(rk), r3(rv), r3(rg), w["ret_norm"], ret_s0, layer, c=cfg["c_ret"], nsub=cfg["n_ret"])
    h0 = None if ssm_h0 is None else ssm_h0.reshape(ssm_h0.shape[0], b, SSM_GROUPS, -1, SSM_STATE)
    o_ssm, h_ssm = _ssd_call(r3(z), r3(xbc), r3(small), w, conv0, h0, layer, c=cfg["c_ssd"], nsub=cfg["n_ssd"])
    if past is None:
        o_mla = _mla_prompt_call(r3(mq), r3(kk), r3(vv), tq=cfg["tq"])
    else:
        k_past, v_past, q_pos, past_pos = past
        o_mla = _mla_sample_call(r3(mq), k_past, v_past, r3(kk), r3(vv), q_pos, past_pos)
    f2 = lambda a: a.reshape(n, a.shape[-1])
    x3 = _ffn_call(x1, [f2(o_ret), f2(o_mla), f2(o_ssm)], wb["w_out"], w["ffn2_norm"], wb["ffn2_wgu"], wb["ffn2_wd"],
                   layer, tm=cfg["tm_ffn"])
    keep = CONV_W - 1
    if t >= keep:
        conv_new = r3(xbc)[:, t - keep:]
    else:
        conv_new = jnp.concatenate([conv0[:, CONV_PAD - keep:], r3(xbc)], axis=1)[:, t:]
    state = (r3(ckv), r3(small)[:, :, KROPE_LO:KROPE_LO + MLA_ROPE], s_ret,
             h_ssm.reshape(b, SSM_HEADS, SSM_HEADDIM, SSM_STATE), conv_new)
    return x3.reshape(b, t, D_MODEL), state


PROMPT_CFG = dict(tm_ffn=512, tm_in=512, c_ret=128, n_ret=4, c_ssd=128, n_ssd=4, tq=512)
SAMPLE_CFG = dict(tm_ffn=256, tm_in=256, c_ret=128, n_ret=1, c_ssd=128, n_ssd=1, tq=256)


def kernel(x_prompt, x_sample, cache_mla_ckv, cache_mla_krope, state_ret, state_ssm, state_conv, ffn1_norm, ffn1_wgu, ffn1_wd, mix_norm, w_in, ret_norm, mla_q_norm, mla_kv_norm, mla_w_uq, mla_w_ukv, mla_q_gain, mla_k_gain, ssm_conv_w, ssm_conv_b, ssm_dt_bias, ssm_a_log, ssm_d, ssm_norm, w_out, ffn2_norm, ffn2_wgu, ffn2_wd):
    p = dict(ffn1_norm=ffn1_norm, ffn1_wgu=ffn1_wgu, ffn1_wd=ffn1_wd, mix_norm=mix_norm, w_in=w_in, ret_norm=ret_norm,
             mla_q_norm=mla_q_norm, mla_kv_norm=mla_kv_norm, mla_w_uq=mla_w_uq, mla_w_ukv=mla_w_ukv,
             mla_q_gain=mla_q_gain, mla_k_gain=mla_k_gain, ssm_conv_w=ssm_conv_w, ssm_conv_b=ssm_conv_b,
             ssm_dt_bias=ssm_dt_bias, ssm_a_log=ssm_a_log, ssm_d=ssm_d, ssm_norm=ssm_norm, w_out=w_out,
             ffn2_norm=ffn2_norm, ffn2_wgu=ffn2_wgu, ffn2_wd=ffn2_wd)
    depth = ffn1_norm.shape[0]
    b_s, t_s, _ = x_sample.shape
    t_p = x_prompt.shape[1]
    past = cache_mla_ckv.shape[2]
    q_pos_s = past + np.arange(t_s)
    past_pos = np.arange(past)
    tab_p = jnp.asarray(_rope_tables(np.arange(t_p)))
    tab_s = jnp.asarray(np.tile(_rope_tables(q_pos_s), (b_s, 1)))
    tab_past = jnp.asarray(_rope_tables(past_pos)[:, 2 * LANES:])
    wb = {k: _bf(p[k]) for k in ("ffn1_wgu", "ffn1_wd", "ffn2_wgu", "ffn2_wd", "w_out")}
    yp, ys = x_prompt, x_sample
    new_p, new_s = [[] for _ in range(5)], [[] for _ in range(5)]
    for l in range(depth):
        w = _layer_weights(l, p)
        yp, st_p = _layer(yp, tab_p, w, wb, l, ret_s0=None, ssm_h0=None, conv_buf=None, past=None, cfg=PROMPT_CFG)
        k_past, v_past = _kvprep_call(cache_mla_ckv, cache_mla_krope, l, w["w_kv"], w["k_gain"], tab_past, tm=512)
        ys, st_s = _layer(ys, tab_s, w, wb, l, ret_s0=state_ret, ssm_h0=state_ssm, conv_buf=state_conv[l],
                          past=(k_past, v_past, q_pos_s, past_pos), cfg=SAMPLE_CFG)
        for i in range(5):
            new_p[i].append(st_p[i])
            new_s[i].append(st_s[i])
    return (yp, ys) + tuple(jnp.stack(a) for a in new_p) + tuple(jnp.stack(a) for a in new_s)
```

```python
import functools

import numpy as np
import jax
import jax.numpy as jnp
from jax import lax
from jax.experimental import pallas as pl
from jax.experimental.pallas import tpu as pltpu

D_MODEL = 1024
D_FF = 2816
CHUNK = 64
EPS = 1e-6
RET_HEADS = 8
RET_DK = 64
RET_DV = 64
MLA_HEADS = 8
MLA_Q_LORA = 384
MLA_KV_LORA = 256
MLA_NOPE = 64
MLA_ROPE = 32
MLA_DQK = MLA_NOPE + MLA_ROPE
MLA_DV = 64
ROPE_BASE = 10000.0
SSM_HEADS = 16
SSM_HEADDIM = 64
SSM_D_INNER = SSM_HEADS * SSM_HEADDIM
SSM_GROUPS = 2
SSM_STATE = 128
CONV_W = 4
CONV_DIM = SSM_D_INNER + 2 * SSM_GROUPS * SSM_STATE
RET_COLS = 2 * RET_HEADS * RET_DK + 2 * RET_HEADS * RET_DV
MLA_COLS = MLA_Q_LORA + MLA_KV_LORA + MLA_ROPE

LANES = 128
HALF = LANES // 2
N_PAIRS = 4
RET_GROUPS = 2
RET_GH = RET_HEADS // RET_GROUPS
RET_GW = RET_GH * RET_DK
LOG2E = 1.4426950408889634
CONV_PAD = 8
DT_LANES = SSM_HEADS
KROPE_LO = MLA_NOPE
V7X_VMEM_LIMIT_BYTES = 56 * 1024 * 1024

F32 = jnp.float32
BF16 = jnp.bfloat16


def _bf(x):
    return x.astype(BF16)


def _dot(a, b):
    return jnp.dot(a, b, preferred_element_type=F32)


def _dot_nt(a, b):
    return lax.dot_general(a, b, (((1,), (1,)), ((), ())), preferred_element_type=F32)


def _sigmoid(x):
    return 1.0 / (1.0 + jnp.exp(-x))


def _silu(x):
    return x * _sigmoid(x)


def _rms(x):
    return x * lax.rsqrt(jnp.mean(x * x, axis=-1, keepdims=True) + EPS)


def _split3(x):
    hi = _bf(x)
    r = x - hi.astype(F32)
    mid = _bf(r)
    lo = _bf(r - mid.astype(F32))
    return hi, mid, lo


def _exact01(m01, x):
    hi, mid, lo = _split3(x)
    return _dot(m01, hi) + _dot(m01, mid) + _dot(m01, lo)


def _exact01_nt(m01, x):
    hi, mid, lo = _split3(x)
    return _dot_nt(m01, hi) + _dot_nt(m01, mid) + _dot_nt(m01, lo)


def _exact01_tn(x, m01):
    hi, mid, lo = _split3(x)
    return _dot(hi, m01) + _dot(mid, m01) + _dot(lo, m01)


def _dot_hi_mid(x, m01):
    hi = _bf(x)
    return _dot(hi, m01) + _dot(_bf(x - hi.astype(F32)), m01)


def _lane_lo(rows):
    return lax.broadcasted_iota(jnp.int32, (rows, LANES), 1) < HALF


def _conv_silu_slab(xbuf, j, x_new, cw, cb):
    rows = x_new.shape[0]
    xbuf[j, CONV_PAD:CONV_PAD + rows, :] = x_new
    conv = cb
    for wi in range(CONV_W):
        off = CONV_PAD - (CONV_W - 1) + wi
        conv = conv + cw[wi:wi + 1, :] * xbuf[j, off:off + rows, :]
    xbuf[j, 0:CONV_PAD, :] = xbuf[j, rows:rows + CONV_PAD, :]
    return _silu(conv)


def _const_spec(shape):
    nd = len(shape)
    return pl.BlockSpec(shape, lambda *_: (0,) * nd, pipeline_mode=pl.Buffered(1))


def _params(n_axes):
    return pltpu.CompilerParams(dimension_semantics=("arbitrary",) * n_axes,
                                vmem_limit_bytes=V7X_VMEM_LIMIT_BYTES)


def _ffn_body(*refs, n_pre, fc):
    x_ref = refs[0]
    pre = [(refs[1 + 2 * i], refs[2 + 2 * i]) for i in range(n_pre)]
    g_ref, wg_ref, wu_ref, wd_ref, o_ref = refs[1 + 2 * n_pre:]
    x = x_ref[...]
    for m_ref, w_ref in pre:
        x = x + _dot(m_ref[...], w_ref[...])
    xn = _bf(_rms(x) * g_ref[...])
    acc = None
    for c in range(D_FF // fc):
        gate = _dot(xn, wg_ref[:, c * fc:(c + 1) * fc])
        up = _dot(xn, wu_ref[:, c * fc:(c + 1) * fc])
        part = _dot(_bf(_silu(gate) * up), wd_ref[c * fc:(c + 1) * fc, :])
        acc = part if acc is None else acc + part
    o_ref[...] = x + 0.5 * acc


def _resident_block(block, index):
    return pl.BlockSpec(block, lambda *_: index, pipeline_mode=pl.Buffered(1))


def _ffn_call(x, pre, w_out, gain, wgu, wd, layer, *, tm, fc=256):
    n = x.shape[0]
    tm = min(tm, n)
    assert n % tm == 0 and D_FF % fc == 0
    args, specs = [x], [pl.BlockSpec((tm, D_MODEL), lambda i: (i, 0))]
    row = 0
    for m in pre:
        width = m.shape[1]
        assert row % width == 0
        args += [m, w_out]
        specs += [pl.BlockSpec((tm, width), lambda i: (i, 0)),
                  _resident_block((None, width, D_MODEL), (layer, row // width, 0))]
        row += width
    args += [gain, wgu, wgu, wd]
    specs += [_const_spec(gain.shape), _resident_block((None, D_MODEL, D_FF), (layer, 0, 0)),
              _resident_block((None, D_MODEL, D_FF), (layer, 0, 1)), _resident_block((None, D_FF, D_MODEL), (layer, 0, 0))]
    return pl.pallas_call(
        functools.partial(_ffn_body, n_pre=len(pre), fc=fc),
        grid=(n // tm,),
        in_specs=specs,
        out_specs=pl.BlockSpec((tm, D_MODEL), lambda i: (i, 0)),
        out_shape=jax.ShapeDtypeStruct((n, D_MODEL), F32),
        compiler_params=_params(1),
        name="ffn",
    )(*args)


def _kv_from_latent(ckv_n, krope_slab, krope_sw_slab, wkv_ref, kgain, kgain_sw, cos, sin_s, k_ref, v_ref):
    kvp = _dot(_bf(ckv_n), wkv_ref[...])
    ones_hi = jnp.where(_lane_lo(kvp.shape[0]), 0.0, 1.0)
    g_cos = kgain * cos
    rot = krope_sw_slab * (kgain_sw * sin_s)
    for h in range(MLA_HEADS):
        kh = kvp[:, h * LANES:(h + 1) * LANES] + krope_slab
        ss = jnp.sum(kh * kh, axis=-1, keepdims=True) * (1.0 / MLA_DQK)
        k_ref[:, h * LANES:(h + 1) * LANES] = _bf((kh * g_cos + rot) * lax.rsqrt(ss + EPS))
        vs = slice((MLA_HEADS + h) * LANES, (MLA_HEADS + h + 1) * LANES)
        v_ref[:, h * LANES:(h + 1) * LANES] = _bf(kvp[:, vs] + ones_hi)


def _kvprep_body(ckv_ref, kr_ref, place_ref, wkv_ref, kgain_ref, tab_ref, k_ref, v_ref):
    slabs = _exact01_tn(kr_ref[...], place_ref[...])
    tab = tab_ref[...]
    kgain = kgain_ref[...]
    _kv_from_latent(ckv_ref[...], slabs[:, :LANES], slabs[:, LANES:], wkv_ref, kgain[0:1], kgain[1:2],
                    tab[:, :LANES], tab[:, LANES:], k_ref, v_ref)


def _swap_lanes():
    idx = np.arange(LANES)
    half = MLA_ROPE // 2
    idx[KROPE_LO:KROPE_LO + half] += half
    idx[KROPE_LO + half:KROPE_LO + MLA_ROPE] -= half
    return idx


def _kvprep_call(ckv, krope, layer, wkv, kgain, tab, *, tm):
    _, b, s, _ = ckv.shape
    tm = min(tm, s)
    assert s % tm == 0
    place = np.zeros((MLA_ROPE, 2 * LANES), np.float32)
    place[np.arange(MLA_ROPE), KROPE_LO + np.arange(MLA_ROPE)] = 1.0
    place[np.arange(MLA_ROPE), LANES + _swap_lanes()[KROPE_LO:KROPE_LO + MLA_ROPE]] = 1.0
    place = jnp.asarray(place, BF16)
    return pl.pallas_call(
        _kvprep_body,
        grid=(b, s // tm),
        in_specs=[pl.BlockSpec((None, None, tm, MLA_KV_LORA), lambda i, j: (layer, i, j, 0)),
                  pl.BlockSpec((None, None, tm, MLA_ROPE), lambda i, j: (layer, i, j, 0)),
                  _const_spec(place.shape), _const_spec(wkv.shape), _const_spec(kgain.shape),
                  pl.BlockSpec((tm, 2 * LANES), lambda i, j: (j, 0))],
        out_specs=[pl.BlockSpec((None, tm, MLA_HEADS * LANES), lambda i, j: (i, j, 0))] * 2,
        out_shape=[jax.ShapeDtypeStruct((b, s, MLA_HEADS * LANES), BF16)] * 2,
        compiler_params=_params(2),
        name="kvprep",
    )(ckv, krope, place, wkv, kgain, tab)


def _inproj_body(x_ref, tab_ref, gmix_ref, wret_ref, wcq_ref, wckv_ref, wz_ref, wxbc_ref, wsm_ref,
                 gq_ref, gkv_ref, wuq_ref, qgain_ref, wkv_ref, kgain_ref,
                 rq_ref, rk_ref, rv_ref, rg_ref, mq_ref, ckv_ref, sm_ref, z_ref, xbc_ref, k_ref, v_ref):
    hn = _bf(_rms(x_ref[...]) * gmix_ref[...])
    tab = tab_ref[...]
    cos_r, sin_r = tab[:, 0:LANES], tab[:, LANES:2 * LANES]
    cos_m, sin_s = tab[:, 2 * LANES:3 * LANES], tab[:, 3 * LANES:4 * LANES]

    xbc_ref[...] = _dot(hn, wxbc_ref[...])
    z_ref[...] = _silu(_dot(hn, wz_ref[...]))

    hk = RET_HEADS * RET_DK
    ret = _dot(hn, wret_ref[...])
    for out_ref, base, scale in ((rq_ref, 0, 1.0), (rk_ref, hk, RET_DK ** -0.5)):
        for g in range(RET_GROUPS):
            lo_sl = slice(g * RET_GW, g * RET_GW + LANES)
            hi_sl = slice(g * RET_GW + LANES, (g + 1) * RET_GW)
            x1 = ret[:, base + g * RET_GW:base + g * RET_GW + LANES]
            x2 = ret[:, base + g * RET_GW + LANES:base + (g + 1) * RET_GW]
            out_ref[:, lo_sl] = _bf((x1 * cos_r - x2 * sin_r) * scale)
            out_ref[:, hi_sl] = _bf((x1 * sin_r + x2 * cos_r) * scale)
    rv_ref[...] = _bf(ret[:, 2 * hk:3 * hk])
    rg_ref[...] = _silu(ret[:, 3 * hk:])

    cq = _bf(_rms(_dot(hn, wcq_ref[...])) * gq_ref[...])
    qp = _dot(cq, wuq_ref[...])
    qgain = qgain_ref[...]
    q_scale = (MLA_DQK ** -0.5) * LOG2E
    g_cos = qgain[0:1] * cos_m * q_scale
    g_sin = qgain[1:2] * sin_s * q_scale
    for h in range(MLA_HEADS):
        qh = qp[:, h * LANES:(h + 1) * LANES]
        qs = qp[:, (MLA_HEADS + h) * LANES:(MLA_HEADS + h + 1) * LANES]
        ss = jnp.sum(qh * qh, axis=-1, keepdims=True) * (1.0 / MLA_DQK)
        mq_ref[:, h * LANES:(h + 1) * LANES] = _bf((qh * g_cos + qs * g_sin) * lax.rsqrt(ss + EPS))

    small2 = _dot(hn, wsm_ref[...])
    small = small2[:, :LANES]
    sm_ref[...] = small
    ckv_n = _rms(_dot(hn, wckv_ref[...])) * gkv_ref[...]
    ckv_ref[...] = ckv_n
    lane = lax.broadcasted_iota(jnp.int32, small.shape, 1)
    rot_lanes = (lane >= KROPE_LO) & (lane < KROPE_LO + MLA_ROPE)
    kgain = kgain_ref[...]
    _kv_from_latent(ckv_n, jnp.where(rot_lanes, small, 0.0), jnp.where(rot_lanes, small2[:, LANES:], 0.0),
                    wkv_ref, kgain[0:1], kgain[1:2], cos_m, sin_s, k_ref, v_ref)


def _inproj_call(x, tab, w, *, tm, t_per_batch):
    n = x.shape[0]
    tm = min(tm, n)
    assert n % tm == 0
    if tm <= t_per_batch:
        assert t_per_batch % tm == 0 and tab.shape[0] == t_per_batch
        per = t_per_batch // tm
        tab_spec = pl.BlockSpec((tm, 4 * LANES), lambda i: (i % per, 0))
    else:
        assert tab.shape[0] == tm
        tab_spec = pl.BlockSpec((tm, 4 * LANES), lambda i: (0, 0))
    consts = [w["mix_norm"], w["w_ret"], w["w_cq"], w["w_ckv"], w["w_z"], w["w_xbc"], w["w_small"],
              w["q_norm"], w["kv_norm"], w["w_uq"], w["q_gain"], w["w_kv"], w["k_gain"]]
    widths = [(512, BF16), (512, BF16), (512, BF16), (512, F32), (MLA_HEADS * LANES, BF16),
              (MLA_KV_LORA, F32), (LANES, F32), (SSM_D_INNER, F32), (CONV_DIM, F32),
              (MLA_HEADS * LANES, BF16), (MLA_HEADS * LANES, BF16)]
    return pl.pallas_call(
        _inproj_body,
        grid=(n // tm,),
        in_specs=[pl.BlockSpec((tm, D_MODEL), lambda i: (i, 0)), tab_spec] + [_const_spec(c.shape) for c in consts],
        out_specs=[pl.BlockSpec((tm, wd), lambda i: (i, 0)) for wd, _ in widths],
        out_shape=[jax.ShapeDtypeStruct((n, wd), dt) for wd, dt in widths],
        compiler_params=_params(1),
        name="inproj",
    )(x, tab, *consts)


def _ret_body(*refs, has_s0, c, nsub):
    (q_ref, k_ref, v_ref, g_ref, dm_ref, qd_ref, kd_ref, cd_ref, bd_ref, eye_ref, gn_ref, hm_ref, vm_ref) = refs[:13]
    s0_ref = refs[13] if has_s0 else None
    o_ref, sout_ref, s_ref = refs[-3:]
    half_dk = RET_DK // 2

    def state_blocks():
        for g in range(RET_GROUPS):
            for hh in range(RET_GH):
                cols = slice(hh * RET_DV, (hh + 1) * RET_DV)
                for part in range(2):
                    r0 = part * LANES + hh * half_dk
                    yield g * RET_GH + hh, slice(part * half_dk, (part + 1) * half_dk), g, slice(r0, r0 + half_dk), cols

    @pl.when(pl.program_id(1) == 0)
    def _():
        s_ref[...] = jnp.zeros(s_ref.shape, F32)
        if has_s0:
            for h, hrows, g, rows_, cols in state_blocks():
                s_ref[g, rows_, cols] = s0_ref[h, hrows, :]

    lo = _lane_lo(c)
    eye = eye_ref[...]
    for sub in range(nsub):
        rs = slice(sub * c, (sub + 1) * c)
        for g in range(RET_GROUPS):
            gs = slice(g * RET_GW, (g + 1) * RET_GW)
            qg, kg, vg = q_ref[rs, gs], k_ref[rs, gs], v_ref[rs, gs]
            state = s_ref[g]
            o = _dot(qg, _bf(state)) * qd_ref[:, gs]
            for pp in range(RET_GH // 2):
                h0, h1 = 2 * pp, 2 * pp + 1
                k2 = jnp.concatenate([kg * hm_ref[h0], kg * hm_ref[h1]], axis=0)
                v2 = jnp.concatenate([vg * vm_ref[h0], vg * vm_ref[h1]], axis=0)
                att = _dot_nt(qg, k2) * dm_ref[g * (RET_GH // 2) + pp]
                o = o + _dot(_bf(att), v2)
            for half in range(2):
                sl = slice(g * RET_GW + half * LANES, g * RET_GW + (half + 1) * LANES)
                oh = o[:, half * LANES:(half + 1) * LANES]
                oo = oh * oh
                s_lo = jnp.sum(jnp.where(lo, oo, 0.0), axis=-1, keepdims=True)
                s_hi = jnp.sum(jnp.where(lo, 0.0, oo), axis=-1, keepdims=True)
                ss = jnp.where(lo, s_lo, s_hi) * (1.0 / RET_DV)
                o_ref[rs, sl] = _bf(oh * lax.rsqrt(ss + EPS) * gn_ref[:, sl] * g_ref[rs, sl])
            kdec = _bf(kg.astype(F32) * kd_ref[:, gs])
            kdec_t = _bf(_dot_nt(eye, kdec))
            s_ref[g] = state * cd_ref[g] + _dot(kdec_t, vg) * bd_ref[...]

    @pl.when(pl.program_id(1) == pl.num_programs(1) - 1)
    def _():
        for h, hrows, g, rows_, cols in state_blocks():
            sout_ref[h, hrows, :] = s_ref[g, rows_, cols]


def _ret_qk_head(width):
    return (np.arange(width) % LANES) // (RET_DK // 2) + RET_GH * (np.arange(width) // RET_GW)


def _ret_tables(c):
    log_g = jnp.log(1.0 - 2.0 ** (-5.0 - jnp.arange(RET_HEADS, dtype=F32)))
    idx = jnp.arange(c, dtype=F32)
    diff = idx[:, None] - idx[None, :]
    dmask = jnp.where(diff >= 0, jnp.exp(log_g[:, None, None] * jnp.maximum(diff, 0.0)), 0.0)
    dmask = jnp.concatenate([dmask[0::2], dmask[1::2]], axis=2)
    q_dec = jnp.exp(log_g[:, None] * (idx[None, :] + 1.0))
    k_dec = jnp.exp(log_g[:, None] * (c - 1.0 - idx[None, :]))
    c_dec = jnp.exp(log_g * c)
    w = RET_HEADS * RET_DK
    qk_head = _ret_qk_head(w)
    v_head = np.arange(w) // RET_DV
    qd = q_dec.T[:, v_head]
    kd = k_dec.T[:, qk_head]
    cd = jnp.broadcast_to(c_dec[qk_head].reshape(RET_GROUPS, RET_GW, 1), (RET_GROUPS, RET_GW, RET_GW))
    bd = jnp.asarray((qk_head[:RET_GW, None] == v_head[None, :RET_GW]).astype(np.float32))
    hm = np.stack([np.broadcast_to(qk_head[:RET_GW] == hh, (c, RET_GW)) for hh in range(RET_GH)])
    vm = np.stack([np.broadcast_to(v_head[:RET_GW] == hh, (c, RET_GW)) for hh in range(RET_GH)])
    return dmask, qd, kd, cd, bd, jnp.asarray(hm, BF16), jnp.asarray(vm, BF16)


def _ret_call(rq, rk, rv, rg, gn, s0, layer, *, c, nsub):
    b, t, w = rq.shape
    c = min(c, t)
    nsub = min(nsub, t // c)
    blk = c * nsub
    assert t % blk == 0
    dmask, qd, kd, cd, bd, hm, vm = _ret_tables(c)
    eye = jnp.eye(RET_GW, dtype=BF16)
    tok = lambda i, j: (i, j, 0)
    args = [rq, rk, rv, rg, dmask, qd, kd, cd, bd, eye, gn, hm, vm]
    specs = [pl.BlockSpec((None, blk, w), tok)] * 4 + [_const_spec(a.shape) for a in args[4:]]
    st_spec = pl.BlockSpec((None, RET_HEADS, RET_DK, RET_DV), lambda i, j: (i, 0, 0, 0))
    if s0 is not None:
        args.append(s0)
        specs.append(pl.BlockSpec((None, None, RET_HEADS, RET_DK, RET_DV), lambda i, j: (layer, i, 0, 0, 0)))
    return pl.pallas_call(
        functools.partial(_ret_body, has_s0=s0 is not None, c=c, nsub=nsub),
        grid=(b, t // blk),
        in_specs=specs,
        out_specs=[pl.BlockSpec((None, blk, w), tok), st_spec],
        out_shape=[jax.ShapeDtypeStruct((b, t, w), BF16),
                   jax.ShapeDtypeStruct((b, RET_HEADS, RET_DK, RET_DV), F32)],
        scratch_shapes=[pltpu.VMEM((RET_GROUPS, RET_GW, RET_GW), F32)],
        compiler_params=_params(2),
        name="retention",
    )(*args)


def _ssd_body(*refs, has_h0, c, nsub):
    (z_ref, xbc_ref, sm_ref, cw_ref, cb_ref, dtb_ref, alog_ref, dsk_ref, ng_ref,
     tri_ref, eye_ref, spread_ref, conv0_ref) = refs[:13]
    h0_ref = refs[13] if has_h0 else None
    o_ref, h_ref, xbuf, st = refs[-4:]
    rows = z_ref.shape[0]
    ci = pl.program_id(1)

    n_slabs = CONV_DIM // LANES

    @pl.when(ci == 0)
    def _():
        for j in range(n_slabs):
            xbuf[j, 0:CONV_PAD, :] = conv0_ref[:, j * LANES:(j + 1) * LANES]
        for g in range(SSM_GROUPS):
            st[g] = h0_ref[g].T if has_h0 else jnp.zeros(st.shape[1:], F32)

    act_slabs = []
    for j in range(n_slabs):
        sl = slice(j * LANES, (j + 1) * LANES)
        act_slabs.append(_conv_silu_slab(xbuf, j, xbc_ref[:, sl], cw_ref[:, sl], cb_ref[:, sl]))
    x_slab0, b_slab0, c_slab0 = 0, SSM_D_INNER // LANES, SSM_D_INNER // LANES + SSM_GROUPS

    lane = lax.broadcasted_iota(jnp.int32, (rows, LANES), 1)
    lane1 = lax.broadcasted_iota(jnp.int32, (1, LANES), 1)
    dt_raw = jnp.where(lane < DT_LANES, sm_ref[...], 0.0) + dtb_ref[...]
    dt_all = jnp.maximum(dt_raw, 0.0) + jnp.log1p(jnp.exp(-jnp.abs(dt_raw)))
    a_neg = jnp.where(lane1 < DT_LANES, -jnp.exp(alog_ref[...]), 0.0)
    dta_all = dt_all * a_neg
    ldt_all = jnp.maximum(jnp.log(dt_all), -1e30)
    eye = eye_ref[...]
    lo = _lane_lo(c)
    lo1 = lane1 < HALF
    m_lo = _bf(jnp.where(lo, 1.0, 0.0))
    m_hi = _bf(jnp.where(lo, 0.0, 1.0))
    r_i = lax.broadcasted_iota(jnp.int32, (c, c), 0)
    c_i = lax.broadcasted_iota(jnp.int32, (c, c), 1)
    causal = r_i >= c_i

    def pair_lanes(v, h):
        return jnp.where(lo1, v[:, h:h + 1], v[:, h + 1:h + 2])

    for sub in range(nsub):
        r0 = sub * c
        rs_ = slice(r0, r0 + c)
        dtv = dt_all[r0:r0 + c]
        a_cum = _exact01(tri_ref[...], dta_all[r0:r0 + c])
        a_col = a_cum * LOG2E
        row_t = _exact01_nt(eye, (a_cum - ldt_all[r0:r0 + c]) * LOG2E)
        a_last = a_cum[c - 1:c, :]
        e_last = jnp.exp(a_last)
        exp_a = _dot_hi_mid(jnp.exp(a_cum), spread_ref[...])
        w_dt = _dot_hi_mid(jnp.exp(a_last - a_cum) * dtv, spread_ref[...])
        for g in range(SSM_GROUPS):
            bmb = _bf(act_slabs[b_slab0 + g][r0:r0 + c])
            cmb = _bf(act_slabs[c_slab0 + g][r0:r0 + c])
            scores = _dot_nt(cmb, bmb)
            y_off = _dot(cmb, _bf(st[g]))
            ssq = jnp.zeros((c, 1), F32)
            gated, xw, e_row = [], [], []
            for jj in range(N_PAIRS):
                j = g * N_PAIRS + jj
                sl = slice(j * LANES, (j + 1) * LANES)
                h0 = 2 * j
                xsl = act_slabs[x_slab0 + j][r0:r0 + c]
                xb = _bf(xsl)
                y = y_off[:, jj * LANES:(jj + 1) * LANES] * exp_a[:, sl] + dsk_ref[:, sl] * xsl
                for e, msk in enumerate((m_lo, m_hi)):
                    h = h0 + e
                    seg = jnp.where(causal, a_col[:, h:h + 1] - row_t[h:h + 1, :], -jnp.inf)
                    y = y + _dot(_bf(scores * jnp.exp2(seg)), xb * msk)
                zs = z_ref[rs_, sl]
                gt = y * zs
                ssq = ssq + jnp.sum(gt * gt, axis=-1, keepdims=True)
                gated.append(gt)
                xw.append(_bf(xsl * w_dt[:, sl]))
                e_row.append(pair_lanes(e_last, h0))
            rs = lax.rsqrt(ssq * (1.0 / (SSM_D_INNER // SSM_GROUPS)) + EPS)
            for jj in range(N_PAIRS):
                sl = slice((g * N_PAIRS + jj) * LANES, (g * N_PAIRS + jj + 1) * LANES)
                o_ref[rs_, sl] = _bf(gated[jj] * rs * ng_ref[:, sl])
            bm_t = _bf(_dot_nt(eye, bmb))
            upd = _dot(bm_t, jnp.concatenate(xw, axis=1))
            st[g] = st[g] * jnp.concatenate(e_row, axis=1) + upd

    @pl.when(ci == pl.num_programs(1) - 1)
    def _():
        for g in range(SSM_GROUPS):
            h_ref[g] = st[g].T


def _ssd_call(z, xbc, small, w, conv0, h0, layer, *, c, nsub):
    b, t, _ = z.shape
    c = min(c, t)
    nsub = min(nsub, t // c)
    blk = c * nsub
    assert t % blk == 0 and c >= CONV_PAD
    tri = jnp.asarray(np.tril(np.ones((c, c), np.float32)), BF16)
    eye = jnp.eye(LANES, dtype=BF16)
    spread = np.zeros((LANES, SSM_D_INNER), np.float32)
    spread[np.arange(SSM_D_INNER) // SSM_HEADDIM, np.arange(SSM_D_INNER)] = 1.0
    spread = jnp.asarray(spread, BF16)
    gw = SSM_D_INNER // SSM_GROUPS
    tok = lambda i, j: (i, j, 0)
    consts = [w["conv_w"], w["conv_b"], w["dt_bias"], w["a_log"], w["d_skip"], w["ssm_norm"], tri, eye, spread]
    args = [z, xbc, small] + consts + [conv0]
    specs = ([pl.BlockSpec((None, blk, SSM_D_INNER), tok), pl.BlockSpec((None, blk, CONV_DIM), tok),
              pl.BlockSpec((None, blk, LANES), tok)] + [_const_spec(a.shape) for a in consts]
             + [pl.BlockSpec((None, CONV_PAD, CONV_DIM), lambda i, j: (i, 0, 0))])
    if h0 is not None:
        args.append(h0)
        specs.append(pl.BlockSpec((None, None, SSM_GROUPS, gw, SSM_STATE), lambda i, j: (layer, i, 0, 0, 0)))
    return pl.pallas_call(
        functools.partial(_ssd_body, has_h0=h0 is not None, c=c, nsub=nsub),
        grid=(b, t // blk),
        in_specs=specs,
        out_specs=[pl.BlockSpec((None, blk, SSM_D_INNER), tok),
                   pl.BlockSpec((None, SSM_GROUPS, gw, SSM_STATE), lambda i, j: (i, 0, 0, 0))],
        out_shape=[jax.ShapeDtypeStruct((b, t, SSM_D_INNER), BF16),
                   jax.ShapeDtypeStruct((b, SSM_GROUPS, gw, SSM_STATE), F32)],
        scratch_shapes=[pltpu.VMEM((CONV_DIM // LANES, CONV_PAD + blk, LANES), F32),
                        pltpu.VMEM((SSM_GROUPS, SSM_STATE, gw), F32)],
        compiler_params=_params(2),
        name="ssd",
    )(*args)


def _pair_output(acc0, acc1, lo):
    return jnp.where(lo, acc0 / pltpu.roll(acc0, HALF, 1), pltpu.roll(acc1, HALF, 1) / acc1)


def _mla_prompt_body(q_ref, k_ref, v_ref, o_ref, m_sc, acc_sc, *, tq):
    qi = pl.program_id(1)
    m_sc[...] = jnp.full(m_sc.shape, -jnp.inf, F32)
    acc_sc[...] = jnp.zeros(acc_sc.shape, F32)

    def step(off, row0, ncols, visible):
        rows = slice(row0, tq)
        for h in range(MLA_HEADS):
            hs = slice(h * LANES, (h + 1) * LANES)
            s = _dot_nt(q_ref[rows, hs], k_ref[pl.ds(off, ncols), hs])
            if visible is not None:
                s = jnp.where(visible, s, -jnp.inf)
            m_old = m_sc[h, rows]
            m_new = jnp.maximum(m_old, jnp.max(s, axis=-1, keepdims=True))
            alpha = jnp.exp2(m_old - m_new)
            p = jnp.exp2(s - jnp.concatenate([m_new] * (ncols // LANES), axis=1))
            acc_sc[h, rows] = acc_sc[h, rows] * alpha + _dot(_bf(p), v_ref[pl.ds(off, ncols), hs])
            m_sc[h, rows] = m_new

    def body(t, carry):
        step(pl.multiple_of(t * tq, tq), 0, tq, None)
        return carry

    lax.fori_loop(0, qi, body, 0)
    r_i = lax.broadcasted_iota(jnp.int32, (tq, tq), 0)
    c_i = lax.broadcasted_iota(jnp.int32, (tq, tq), 1)
    step(pl.multiple_of(qi * tq, tq), 0, tq, lax.shift_right_logical(c_i, 6) <= lax.shift_right_logical(r_i, 6))
    lo = _lane_lo(tq)
    for j in range(N_PAIRS):
        o_ref[:, j * LANES:(j + 1) * LANES] = _bf(_pair_output(acc_sc[2 * j], acc_sc[2 * j + 1], lo))


def _mla_prompt_call(q, k, v, *, tq):
    b, t, _ = q.shape
    tq = min(tq, t)
    assert t % tq == 0 and tq % LANES == 0 and CHUNK == 64
    kw = MLA_HEADS * LANES
    return pl.pallas_call(
        functools.partial(_mla_prompt_body, tq=tq),
        grid=(b, t // tq),
        in_specs=[pl.BlockSpec((None, tq, kw), lambda i, j: (i, j, 0)),
                  pl.BlockSpec((None, t, kw), lambda i, j: (i, 0, 0)),
                  pl.BlockSpec((None, t, kw), lambda i, j: (i, 0, 0))],
        out_specs=pl.BlockSpec((None, tq, MLA_HEADS * MLA_DV), lambda i, j: (i, j, 0)),
        out_shape=jax.ShapeDtypeStruct((b, t, MLA_HEADS * MLA_DV), BF16),
        scratch_shapes=[pltpu.VMEM((MLA_HEADS, tq, LANES), F32), pltpu.VMEM((MLA_HEADS, tq, LANES), F32)],
        compiler_params=_params(2),
        name="mla_prompt",
    )(q, k, v)


def _mla_sample_body(q_ref, kp_ref, vp_ref, kn_ref, vn_ref, visp_ref, visn_ref, o_ref):
    lo = _lane_lo(q_ref.shape[0])
    vis_p = visp_ref[...] > 0.5
    vis_n = visn_ref[...] > 0.5
    for j in range(N_PAIRS):
        accs = []
        for h in (2 * j, 2 * j + 1):
            hs = slice(h * LANES, (h + 1) * LANES)
            qh = q_ref[:, hs]
            s_p = jnp.where(vis_p, _dot_nt(qh, kp_ref[:, hs]), -jnp.inf)
            s_n = jnp.where(vis_n, _dot_nt(qh, kn_ref[:, hs]), -jnp.inf)
            m = jnp.maximum(jnp.max(s_p, axis=-1, keepdims=True), jnp.max(s_n, axis=-1, keepdims=True))
            accs.append(_dot(_bf(jnp.exp2(s_p - m)), vp_ref[:, hs]) + _dot(_bf(jnp.exp2(s_n - m)), vn_ref[:, hs]))
        o_ref[:, j * LANES:(j + 1) * LANES] = _bf(_pair_output(accs[0], accs[1], lo))


def _mla_sample_call(q, k_past, v_past, k_new, v_new, q_pos, past_pos):
    b, t, _ = q.shape
    s = k_past.shape[1]
    vis_p = jnp.asarray((past_pos[None, :] // CHUNK <= q_pos[:, None] // CHUNK).astype(np.float32))
    vis_n = jnp.asarray((q_pos[None, :] // CHUNK <= q_pos[:, None] // CHUNK).astype(np.float32))
    kw, vw = MLA_HEADS * LANES, MLA_HEADS * MLA_DV
    bat = lambda i: (i, 0, 0)
    return pl.pallas_call(
        _mla_sample_body,
        grid=(b,),
        in_specs=[pl.BlockSpec((None, t, kw), bat), pl.BlockSpec((None, s, kw), bat),
                  pl.BlockSpec((None, s, kw), bat), pl.BlockSpec((None, t, kw), bat),
                  pl.BlockSpec((None, t, kw), bat), _const_spec(vis_p.shape), _const_spec(vis_n.shape)],
        out_specs=pl.BlockSpec((None, t, vw), bat),
        out_shape=jax.ShapeDtypeStruct((b, t, vw), BF16),
        compiler_params=_params(1),
        name="mla_sample",
    )(q, k_past, v_past, k_new, v_new, vis_p, vis_n)


def _rope_tables(pos):
    posf = np.asarray(pos, np.float32)
    th_r = (1.0 / (10000.0 ** np.linspace(0.0, 1.0, RET_DK // 2, dtype=np.float32))).astype(np.float32)
    ang = posf[:, None] * th_r[None, :]
    c, s = np.cos(ang), np.sin(ang)
    ret = [np.concatenate([c] * RET_GH, 1), np.concatenate([s] * RET_GH, 1)]
    th_m = (1.0 / (ROPE_BASE ** (np.arange(0, MLA_ROPE, 2, dtype=np.float32) / MLA_ROPE))).astype(np.float32)
    angm = posf[:, None] * th_m[None, :]
    cm, sm = np.cos(angm), np.sin(angm)
    n = posf.shape[0]
    one, z64 = np.ones((n, MLA_NOPE), np.float32), np.zeros((n, MLA_NOPE), np.float32)
    z32 = np.zeros((n, LANES - MLA_DQK), np.float32)
    mla = [np.concatenate([one, cm, cm, z32], 1), np.concatenate([z64, -sm, sm, z32], 1)]
    return np.concatenate(ret + mla, axis=1).astype(np.float32)


def _layer_weights(l, p):
    w_in = _bf(p["w_in"][l])
    o = RET_COLS
    w_cq, w_ckv = w_in[:, o:o + MLA_Q_LORA], w_in[:, o + MLA_Q_LORA:o + MLA_Q_LORA + MLA_KV_LORA]
    w_kr = w_in[:, o + MLA_Q_LORA + MLA_KV_LORA:o + MLA_COLS]
    o += MLA_COLS
    w_z, w_xbc, w_dt = w_in[:, o:o + SSM_D_INNER], w_in[:, o + SSM_D_INNER:o + SSM_D_INNER + CONV_DIM], w_in[:, o + SSM_D_INNER + CONV_DIM:]
    zc = lambda n: jnp.zeros((D_MODEL, n), BF16)
    half = MLA_ROPE // 2
    w_kr_sw = jnp.concatenate([w_kr[:, half:], w_kr[:, :half]], axis=1)
    small_cols = lambda kr: [w_dt, zc(KROPE_LO - DT_LANES), kr, zc(LANES - KROPE_LO - MLA_ROPE)]
    w_small = jnp.concatenate(small_cols(w_kr) + [zc(KROPE_LO), w_kr_sw, zc(LANES - KROPE_LO - MLA_ROPE)], axis=1)
    swap = _swap_lanes()
    w_uq = jnp.pad(_bf(p["mla_w_uq"][l]).reshape(MLA_Q_LORA, MLA_HEADS, MLA_DQK), ((0, 0), (0, 0), (0, LANES - MLA_DQK)))
    rot = (np.arange(LANES) >= KROPE_LO) & (np.arange(LANES) < KROPE_LO + MLA_ROPE)
    w_uq_sw = jnp.where(jnp.asarray(rot), w_uq[:, :, swap], jnp.zeros((), BF16))
    w_uq = jnp.concatenate([w_uq.reshape(MLA_Q_LORA, -1), w_uq_sw.reshape(MLA_Q_LORA, -1)], axis=1)
    w_ukv = _bf(p["mla_w_ukv"][l]).reshape(MLA_KV_LORA, MLA_HEADS, MLA_NOPE + MLA_DV)
    w_kn = jnp.pad(w_ukv[:, :, :MLA_NOPE], ((0, 0), (0, 0), (0, LANES - MLA_NOPE)))
    w_vp = jnp.pad(w_ukv[:, :, MLA_NOPE:], ((0, 0), (0, 0), (0, LANES - MLA_DV)))
    w_kv = jnp.concatenate([w_kn.reshape(MLA_KV_LORA, -1), w_vp.reshape(MLA_KV_LORA, -1)], axis=1)

    def pad_gain(g):
        gp = jnp.pad(g, (0, LANES - MLA_DQK))
        return jnp.stack([gp, gp[swap]])

    pad16 = lambda v: jnp.pad(v, (0, LANES - DT_LANES)).reshape(1, LANES)
    hk = RET_HEADS * RET_DK

    def group_halves(wq):
        w5 = wq.reshape(D_MODEL, RET_GROUPS, RET_GH, 2, RET_DK // 2)
        return w5.transpose(0, 1, 3, 2, 4).reshape(D_MODEL, hk)

    w_ret = jnp.concatenate([group_halves(w_in[:, :hk]), group_halves(w_in[:, hk:2 * hk]), w_in[:, 2 * hk:RET_COLS]], axis=1)
    out = {
        "mix_norm": p["mix_norm"][l].reshape(1, -1), "w_ret": w_ret, "w_cq": w_cq,
        "w_ckv": w_ckv, "w_z": w_z, "w_xbc": w_xbc, "w_small": w_small,
        "q_norm": p["mla_q_norm"][l].reshape(1, -1), "kv_norm": p["mla_kv_norm"][l].reshape(1, -1),
        "w_uq": w_uq, "q_gain": pad_gain(p["mla_q_gain"][l]),
        "w_kv": w_kv, "k_gain": pad_gain(p["mla_k_gain"][l]),
        "ret_norm": p["ret_norm"][l].reshape(1, -1),
        "conv_w": p["ssm_conv_w"][l], "conv_b": p["ssm_conv_b"][l].reshape(1, -1),
        "dt_bias": pad16(p["ssm_dt_bias"][l]), "a_log": pad16(p["ssm_a_log"][l]),
        "d_skip": jnp.repeat(p["ssm_d"][l], SSM_HEADDIM).reshape(1, -1), "ssm_norm": p["ssm_norm"][l].reshape(1, -1),
        "ffn1_norm": p["ffn1_norm"][l].reshape(1, -1), "ffn2_norm": p["ffn2_norm"][l].reshape(1, -1),
    }
    return out


def _layer(x, tab, w, wb, layer, *, ret_s0, ssm_h0, conv_buf, past, cfg):
    b, t, _ = x.shape
    n = b * t
    x1 = _ffn_call(x.reshape(n, D_MODEL), [], None, w["ffn1_norm"], wb["ffn1_wgu"], wb["ffn1_wd"], layer, tm=cfg["tm_ffn"])
    if conv_buf is None:
        conv0 = jnp.zeros((b, CONV_PAD, CONV_DIM), F32)
    else:
        conv0 = jnp.concatenate([jnp.zeros((b, CONV_PAD - (CONV_W - 1), CONV_DIM), F32), conv_buf], axis=1)
    rq, rk, rv, rg, mq, ckv, small, z, xbc, kk, vv = _inproj_call(x1, tab, w, tm=cfg["tm_in"], t_per_batch=t)
    r3 = lambda a: a.reshape(b, t, a.shape[-1])
    o_ret, s_ret = _ret_call(r3(rq), r3(rk), r3(rv), r3(rg), w["ret_norm"], ret_s0, layer, c=cfg["c_ret"], nsub=cfg["n_ret"])
    h0 = None if ssm_h0 is None else ssm_h0.reshape(ssm_h0.shape[0], b, SSM_GROUPS, -1, SSM_STATE)
    o_ssm, h_ssm = _ssd_call(r3(z), r3(xbc), r3(small), w, conv0, h0, layer, c=cfg["c_ssd"], nsub=cfg["n_ssd"])
    if past is None:
        o_mla = _mla_prompt_call(r3(mq), r3(kk), r3(vv), tq=cfg["tq"])
    else:
        k_past, v_past, q_pos, past_pos = past
        o_mla = _mla_sample_call(r3(mq), k_past, v_past, r3(kk), r3(vv), q_pos, past_pos)
    f2 = lambda a: a.reshape(n, a.shape[-1])
    x3 = _ffn_call(x1, [f2(o_ret), f2(o_mla), f2(o_ssm)], wb["w_out"], w["ffn2_norm"], wb["ffn2_wgu"], wb["ffn2_wd"],
                   layer, tm=cfg["tm_ffn"])
    keep = CONV_W - 1
    if t >= keep:
        conv_new = r3(xbc)[:, t - keep:]
    else:
        conv_new = jnp.concatenate([conv0[:, CONV_PAD - keep:], r3(xbc)], axis=1)[:, t:]
    state = (r3(ckv), r3(small)[:, :, KROPE_LO:KROPE_LO + MLA_ROPE], s_ret,
             h_ssm.reshape(b, SSM_HEADS, SSM_HEADDIM, SSM_STATE), conv_new)
    return x3.reshape(b, t, D_MODEL), state


PROMPT_CFG = dict(tm_ffn=1024, tm_in=512, c_ret=128, n_ret=8, c_ssd=128, n_ssd=8, tq=512)
SAMPLE_CFG = dict(tm_ffn=256, tm_in=256, c_ret=128, n_ret=1, c_ssd=128, n_ssd=1, tq=256)


def kernel(x_prompt, x_sample, cache_mla_ckv, cache_mla_krope, state_ret, state_ssm, state_conv, ffn1_norm, ffn1_wgu, ffn1_wd, mix_norm, w_in, ret_norm, mla_q_norm, mla_kv_norm, mla_w_uq, mla_w_ukv, mla_q_gain, mla_k_gain, ssm_conv_w, ssm_conv_b, ssm_dt_bias, ssm_a_log, ssm_d, ssm_norm, w_out, ffn2_norm, ffn2_wgu, ffn2_wd):
    p = dict(ffn1_norm=ffn1_norm, ffn1_wgu=ffn1_wgu, ffn1_wd=ffn1_wd, mix_norm=mix_norm, w_in=w_in, ret_norm=ret_norm,
             mla_q_norm=mla_q_norm, mla_kv_norm=mla_kv_norm, mla_w_uq=mla_w_uq, mla_w_ukv=mla_w_ukv,
             mla_q_gain=mla_q_gain, mla_k_gain=mla_k_gain, ssm_conv_w=ssm_conv_w, ssm_conv_b=ssm_conv_b,
             ssm_dt_bias=ssm_dt_bias, ssm_a_log=ssm_a_log, ssm_d=ssm_d, ssm_norm=ssm_norm, w_out=w_out,
             ffn2_norm=ffn2_norm, ffn2_wgu=ffn2_wgu, ffn2_wd=ffn2_wd)
    depth = ffn1_norm.shape[0]
    b_s, t_s, _ = x_sample.shape
    t_p = x_prompt.shape[1]
    past = cache_mla_ckv.shape[2]
    q_pos_s = past + np.arange(t_s)
    past_pos = np.arange(past)
    tab_p = jnp.asarray(_rope_tables(np.arange(t_p)))
    tab_s = jnp.asarray(np.tile(_rope_tables(q_pos_s), (b_s, 1)))
    tab_past = jnp.asarray(_rope_tables(past_pos)[:, 2 * LANES:])
    wb = {k: _bf(p[k]) for k in ("ffn1_wgu", "ffn1_wd", "ffn2_wgu", "ffn2_wd", "w_out")}
    yp, ys = x_prompt, x_sample
    new_p, new_s = [[] for _ in range(5)], [[] for _ in range(5)]
    for l in range(depth):
        w = _layer_weights(l, p)
        yp, st_p = _layer(yp, tab_p, w, wb, l, ret_s0=None, ssm_h0=None, conv_buf=None, past=None, cfg=PROMPT_CFG)
        k_past, v_past = _kvprep_call(cache_mla_ckv, cache_mla_krope, l, w["w_kv"], w["k_gain"], tab_past, tm=512)
        ys, st_s = _layer(ys, tab_s, w, wb, l, ret_s0=state_ret, ssm_h0=state_ssm, conv_buf=state_conv[l],
                          past=(k_past, v_past, q_pos_s, past_pos), cfg=SAMPLE_CFG)
        for i in range(5):
            new_p[i].append(st_p[i])
            new_s[i].append(st_s[i])
    return (yp, ys) + tuple(jnp.stack(a) for a in new_p) + tuple(jnp.stack(a) for a in new_s)
```
